```python
import math
import jax, jax.numpy as jnp
from jax import lax
import numpy as np

D_MODEL = 1024
BATCH = 8
SEQ = 2048
DEPTH = 1
DEC_BATCH = 128
DEC_SEQ = 8
PAST_LEN = 8192
PAGE_SIZE = 128

N_HEADS_DA = 8
DA_HEAD = 64
DA_HALF = DA_HEAD // 2
DA_WIDTH = N_HEADS_DA * DA_HEAD
LRU_WIDTH = D_MODEL - DA_WIDTH
LRU_BLOCKS = 8
LRU_BLOCK = LRU_WIDTH // LRU_BLOCKS
LRU_CONV = 4
LRU_C = 8.0
MIX_WIDTH = DA_WIDTH + LRU_WIDTH
IN_WIDTH = 3 * DA_WIDTH + 2 * LRU_WIDTH
N_MEM = 256
N_HEADS_X = 4
X_HEAD = 128
X_WIDTH = N_HEADS_X * X_HEAD
FFN_DIM = 2816
FFN_CONV = 3
Q_BLOCK = 128
EPS = 1e-6

kernel_name = "hymba_diffattn_rglru_convffn_step"

F32 = jnp.float32


def lambda_init(layer):
    return 0.8 - 0.6 * math.exp(-0.3 * layer)


def rmsnorm(x, g):
    xf = x.astype(F32)
    y = xf * lax.rsqrt(jnp.mean(xf * xf, axis=-1, keepdims=True) + EPS)
    return (y * g.astype(F32)).astype(x.dtype)


def causal_dwconv(x_ext, w, b):
    width = w.shape[0]
    t = x_ext.shape[1] - width + 1
    out = b
    for j in range(width):
        out = out + x_ext[:, j:j + t] * w[j]
    return out


def alibi_slopes(n):
    return jnp.exp2(-8.0 * jnp.arange(1, n + 1, dtype=F32) / n)


def diff_lambda(p, lam_init):
    return (jnp.exp(jnp.sum(p['lam_q1'].astype(F32) * p['lam_k1'].astype(F32)))
            - jnp.exp(jnp.sum(p['lam_q2'].astype(F32) * p['lam_k2'].astype(F32))) + lam_init)


def diff_attend(q, k, v, pos_q, pos_k, lam):
    s = jnp.einsum('bqhcd,bkhcd->bhcqk', q.astype(F32), k.astype(F32)) * (DA_HALF ** -0.5)
    dist = (pos_q[:, None] - pos_k[None, :]).astype(F32)
    bias = -alibi_slopes(N_HEADS_DA)[:, None, None] * dist
    s = s + bias[None, :, None]
    causal = (pos_q[:, None] >= pos_k[None, :])[None, None, None]
    s = jnp.where(causal, s, -jnp.inf)
    pr = jax.nn.softmax(s, axis=-1)
    w = pr[:, :, 0] - lam * pr[:, :, 1]
    return jnp.einsum('bhqk,bkhe->bqhe', w, v.astype(F32))


def split_in(xn, p):
    z = xn @ p['w_in']
    return jnp.split(z, [DA_WIDTH, 2 * DA_WIDTH, 3 * DA_WIDTH, 3 * DA_WIDTH + LRU_WIDTH], axis=-1)


def diff_qkv(q, k, v, p):
    b, t = q.shape[:2]
    q = rmsnorm(q.reshape(b, t, N_HEADS_DA, 2, DA_HALF), p['g_q'])
    k = rmsnorm(k.reshape(b, t, N_HEADS_DA, 2, DA_HALF), p['g_k'])
    v = v.reshape(b, t, N_HEADS_DA, DA_HEAD)
    return q, k, v


def diff_post(o, p, lam_init, dtype):
    b, t = o.shape[:2]
    return (rmsnorm(o, p['g_sub']) * (1.0 - lam_init)).reshape(b, t, DA_WIDTH).astype(dtype)


def rglru(xc, h0, p):
    b, t, _ = xc.shape
    xf = xc.astype(F32)
    xb = xf.reshape(b, t, LRU_BLOCKS, LRU_BLOCK)
    r = jax.nn.sigmoid(jnp.einsum('btni,nij->btnj', xb, p['lru_wa'].astype(F32)).reshape(b, t, LRU_WIDTH)
                       + p['lru_ba'].astype(F32))
    i = jax.nn.sigmoid(jnp.einsum('btni,nij->btnj', xb, p['lru_wx'].astype(F32)).reshape(b, t, LRU_WIDTH)
                       + p['lru_bx'].astype(F32))
    log_a = -LRU_C * r * jax.nn.softplus(-p['lru_lambda'].astype(F32))
    a = jnp.exp(log_a)
    gx = jnp.sqrt(-jnp.expm1(2.0 * log_a)) * (i * xf)

    def step(h, inp):
        a_t, g_t = inp
        h = a_t * h + g_t
        return h, h

    h_last, hs = lax.scan(step, h0.astype(F32), (jnp.swapaxes(a, 0, 1), jnp.swapaxes(gx, 0, 1)))
    return jnp.swapaxes(hs, 0, 1), h_last


def recurrent_branch(lx, lg, conv_prev, h0, p):
    ext = jnp.concatenate([conv_prev.astype(lx.dtype), lx], axis=1)
    xc = causal_dwconv(ext, p['lru_conv_w'], p['lru_conv_b'])
    hs, h_last = rglru(xc, h0, p)
    out = hs * jax.nn.gelu(lg.astype(F32))
    return out.astype(lx.dtype), ext[:, -(LRU_CONV - 1):], h_last.astype(lx.dtype)


def mem_kv(mem, p):
    b, m, _ = mem.shape
    mn = rmsnorm(mem, p['norm_mem'])
    k = rmsnorm((mn @ p['w_ck']).reshape(b, m, N_HEADS_X, X_HEAD), p['g_ck'])
    v = (mn @ p['w_cv']).reshape(b, m, N_HEADS_X, X_HEAD)
    return k, v


def cross_attn(xn, mk, mv, p):
    b, t, _ = xn.shape
    q = rmsnorm((xn @ p['w_cq']).reshape(b, t, N_HEADS_X, X_HEAD), p['g_cq'])
    s = jnp.einsum('bqhd,bkhd->bhqk', q.astype(F32), mk.astype(F32)) * (X_HEAD ** -0.5)
    pr = jax.nn.softmax(s, axis=-1)
    o = jnp.einsum('bhqk,bkhd->bqhd', pr, mv.astype(F32)).reshape(b, t, X_WIDTH)
    return o.astype(xn.dtype) @ p['w_co']


def conv_ffn(xn, prev, p):
    u = xn @ p['w_up']
    ext = jnp.concatenate([prev.astype(u.dtype), u], axis=1)
    c = causal_dwconv(ext, p['ffn_conv_w'], p['ffn_conv_b'])
    g, val = jnp.split(c, 2, axis=-1)
    return (jax.nn.silu(g) * val) @ p['w_down'], ext[:, -(FFN_CONV - 1):]


def prompt_layer(x, mem, p, lam_init):
    b, t, _ = x.shape
    xn = rmsnorm(x, p['norm_mix'])
    q, k, v, lx, lg = split_in(xn, p)
    q, k, v = diff_qkv(q, k, v, p)
    lam = diff_lambda(p, lam_init)
    nb = t // Q_BLOCK
    qb = jnp.moveaxis(q.reshape(b, nb, Q_BLOCK, N_HEADS_DA, 2, DA_HALF), 1, 0)
    pos_k = jnp.arange(t)

    def block(args):
        qblk, i = args
        pos_q = i * Q_BLOCK + jnp.arange(Q_BLOCK)
        return diff_attend(qblk, k, v, pos_q, pos_k, lam)

    o = lax.map(block, (qb, jnp.arange(nb)))
    o = jnp.moveaxis(o, 0, 1).reshape(b, t, N_HEADS_DA, DA_HEAD)
    da = diff_post(o, p, lam_init, x.dtype)
    conv_prev = jnp.zeros((b, LRU_CONV - 1, LRU_WIDTH), x.dtype)
    h0 = jnp.zeros((b, LRU_WIDTH), F32)
    lru, conv_state, h_last = recurrent_branch(lx, lg, conv_prev, h0, p)
    x = x + jnp.concatenate([da, lru], axis=-1) @ p['w_o']
    mk, mv = mem_kv(mem, p)
    x = x + cross_attn(rmsnorm(x, p['norm_cross']), mk, mv, p)
    ffn_prev = jnp.zeros((b, FFN_CONV - 1, 2 * FFN_DIM), x.dtype)
    f, ffn_state = conv_ffn(rmsnorm(x, p['norm_ffn']), ffn_prev, p)
    x = x + f
    k_rows = k.reshape(b, t, N_HEADS_DA, DA_HEAD)
    return x, (k_rows, v, mk, mv, conv_state, h_last, ffn_state)


def sample_layer(x, layer, cache_k, cache_v, page_table, mk, mv, conv_prev, h0, ffn_prev, p, lam_init):
    b, t, _ = x.shape
    xn = rmsnorm(x, p['norm_mix'])
    q, k, v, lx, lg = split_in(xn, p)
    q, k, v = diff_qkv(q, k, v, p)
    k_rows = k.reshape(b, t, N_HEADS_DA, DA_HEAD)
    lam = diff_lambda(p, lam_init)
    pos_q = PAST_LEN + jnp.arange(t)
    pos_k = jnp.arange(PAST_LEN + t)

    def one_seq(args):
        q_s, k_s, v_s, pages = args
        kp = cache_k[layer, pages].reshape(PAST_LEN, N_HEADS_DA, DA_HEAD)
        vp = cache_v[layer, pages].reshape(PAST_LEN, N_HEADS_DA, DA_HEAD)
        kk = jnp.concatenate([kp, k_s.astype(kp.dtype)], axis=0).reshape(-1, N_HEADS_DA, 2, DA_HALF)
        vv = jnp.concatenate([vp, v_s.astype(vp.dtype)], axis=0)
        return diff_attend(q_s[None], kk[None], vv[None], pos_q, pos_k, lam)[0]

    o = lax.map(one_seq, (q, k_rows, v, page_table))
    da = diff_post(o, p, lam_init, x.dtype)
    lru, conv_state, h_last = recurrent_branch(lx, lg, conv_prev, h0, p)
    x = x + jnp.concatenate([da, lru], axis=-1) @ p['w_o']
    x = x + cross_attn(rmsnorm(x, p['norm_cross']), mk, mv, p)
    f, ffn_state = conv_ffn(rmsnorm(x, p['norm_ffn']), ffn_prev, p)
    x = x + f
    return x, (k_rows, v, conv_state, h_last, ffn_state)


def setup_inputs(seed: int = 0) -> dict:
    key = jax.random.key(seed)
    ks = jax.random.split(key, 48)

    def nrm(k, shape, scale=1.0):
        return jax.random.normal(k, shape, F32) * scale

    def gain(k, shape):
        return 1.0 + 0.02 * jax.random.normal(k, shape, F32)

    n_pages = PAST_LEN // PAGE_SIZE
    n_phys = (5 * DEC_BATCH * n_pages) // 4
    page_table = jax.random.permutation(ks[0], n_phys)[:DEC_BATCH * n_pages].reshape(DEC_BATCH, n_pages).astype(jnp.int32)
    u = jax.random.uniform(ks[1], (DEPTH, LRU_WIDTH), F32, minval=0.9, maxval=0.999)
    L = DEPTH
    return {
        'x_prompt': nrm(ks[2], (BATCH, SEQ, D_MODEL)),
        'x_sample': nrm(ks[3], (DEC_BATCH, DEC_SEQ, D_MODEL)),
        'mem_prompt': nrm(ks[4], (BATCH, N_MEM, D_MODEL)),
        'cache_k': nrm(ks[5], (L, n_phys, PAGE_SIZE, N_HEADS_DA, DA_HEAD)),
        'cache_v': nrm(ks[6], (L, n_phys, PAGE_SIZE, N_HEADS_DA, DA_HEAD)),
        'page_table': page_table,
        'cache_mem_k': nrm(ks[7], (L, DEC_BATCH, N_MEM, N_HEADS_X, X_HEAD)),
        'cache_mem_v': nrm(ks[8], (L, DEC_BATCH, N_MEM, N_HEADS_X, X_HEAD)),
        'state_lru_conv': nrm(ks[9], (L, DEC_BATCH, LRU_CONV - 1, LRU_WIDTH)),
        'state_lru_h': nrm(ks[10], (L, DEC_BATCH, LRU_WIDTH), 0.5),
        'state_ffn_conv': nrm(ks[11], (L, DEC_BATCH, FFN_CONV - 1, 2 * FFN_DIM), 0.5),
        'norm_mix': gain(ks[12], (L, D_MODEL)),
        'w_in': nrm(ks[13], (L, D_MODEL, IN_WIDTH), D_MODEL ** -0.5),
        'g_q': gain(ks[14], (L, DA_HALF)),
        'g_k': gain(ks[15], (L, DA_HALF)),
        'lam_q1': nrm(ks[16], (L, DA_HALF), 0.1),
        'lam_k1': nrm(ks[17], (L, DA_HALF), 0.1),
        'lam_q2': nrm(ks[18], (L, DA_HALF), 0.1),
        'lam_k2': nrm(ks[19], (L, DA_HALF), 0.1),
        'g_sub': gain(ks[20], (L, DA_HEAD)),
        'lru_conv_w': nrm(ks[21], (L, LRU_CONV, LRU_WIDTH), LRU_CONV ** -0.5),
        'lru_conv_b': nrm(ks[22], (L, LRU_WIDTH), 0.01),
        'lru_wa': nrm(ks[23], (L, LRU_BLOCKS, LRU_BLOCK, LRU_BLOCK), LRU_BLOCK ** -0.5),
        'lru_ba': nrm(ks[24], (L, LRU_WIDTH), 0.01),
        'lru_wx': nrm(ks[25], (L, LRU_BLOCKS, LRU_BLOCK, LRU_BLOCK), LRU_BLOCK ** -0.5),
        'lru_bx': nrm(ks[26], (L, LRU_WIDTH), 0.01),
        'lru_lambda': jnp.log(u) - jnp.log1p(-u),
        'w_o': nrm(ks[27], (L, MIX_WIDTH, D_MODEL), MIX_WIDTH ** -0.5),
        'norm_cross': gain(ks[28], (L, D_MODEL)),
        'norm_mem': gain(ks[29], (L, D_MODEL)),
        'w_cq': nrm(ks[30], (L, D_MODEL, X_WIDTH), D_MODEL ** -0.5),
        'w_ck': nrm(ks[31], (L, D_MODEL, X_WIDTH), D_MODEL ** -0.5),
        'w_cv': nrm(ks[32], (L, D_MODEL, X_WIDTH), D_MODEL ** -0.5),
        'g_cq': gain(ks[33], (L, X_HEAD)),
        'g_ck': gain(ks[34], (L, X_HEAD)),
        'w_co': nrm(ks[35], (L, X_WIDTH, D_MODEL), X_WIDTH ** -0.5),
        'norm_ffn': gain(ks[36], (L, D_MODEL)),
        'w_up': nrm(ks[37], (L, D_MODEL, 2 * FFN_DIM), D_MODEL ** -0.5),
        'ffn_conv_w': nrm(ks[38], (L, FFN_CONV, 2 * FFN_DIM), FFN_CONV ** -0.5),
        'ffn_conv_b': nrm(ks[39], (L, 2 * FFN_DIM), 0.01),
        'w_down': nrm(ks[40], (L, FFN_DIM, D_MODEL), FFN_DIM ** -0.5),
    }


def reference(x_prompt, x_sample, mem_prompt, cache_k, cache_v, page_table, cache_mem_k, cache_mem_v,
              state_lru_conv, state_lru_h, state_ffn_conv, norm_mix, w_in, g_q, g_k, lam_q1, lam_k1,
              lam_q2, lam_k2, g_sub, lru_conv_w, lru_conv_b, lru_wa, lru_ba, lru_wx, lru_bx, lru_lambda,
              w_o, norm_cross, norm_mem, w_cq, w_ck, w_cv, g_cq, g_ck, w_co, norm_ffn, w_up, ffn_conv_w,
              ffn_conv_b, w_down):
    yp, ys = x_prompt, x_sample
    sp_all, ss_all = [], []
    for l in range(DEPTH):
        p = {
            'norm_mix': norm_mix[l], 'w_in': w_in[l], 'g_q': g_q[l], 'g_k': g_k[l],
            'lam_q1': lam_q1[l], 'lam_k1': lam_k1[l], 'lam_q2': lam_q2[l], 'lam_k2': lam_k2[l],
            'g_sub': g_sub[l], 'lru_conv_w': lru_conv_w[l], 'lru_conv_b': lru_conv_b[l],
            'lru_wa': lru_wa[l], 'lru_ba': lru_ba[l], 'lru_wx': lru_wx[l], 'lru_bx': lru_bx[l],
            'lru_lambda': lru_lambda[l], 'w_o': w_o[l], 'norm_cross': norm_cross[l],
            'norm_mem': norm_mem[l], 'w_cq': w_cq[l], 'w_ck': w_ck[l], 'w_cv': w_cv[l],
            'g_cq': g_cq[l], 'g_ck': g_ck[l], 'w_co': w_co[l], 'norm_ffn': norm_ffn[l],
            'w_up': w_up[l], 'ffn_conv_w': ffn_conv_w[l], 'ffn_conv_b': ffn_conv_b[l], 'w_down': w_down[l],
        }
        lam0 = lambda_init(l)
        yp, sp = prompt_layer(yp, mem_prompt, p, lam0)
        ys, ss = sample_layer(ys, l, cache_k, cache_v, page_table, cache_mem_k[l], cache_mem_v[l],
                              state_lru_conv[l], state_lru_h[l], state_ffn_conv[l], p, lam0)
        sp_all.append(sp)
        ss_all.append(ss)
    k_prompt = jnp.stack([s[0] for s in sp_all])
    v_prompt = jnp.stack([s[1] for s in sp_all])
    mem_k_prompt = jnp.stack([s[2] for s in sp_all])
    mem_v_prompt = jnp.stack([s[3] for s in sp_all])
    lru_conv_prompt = jnp.stack([s[4] for s in sp_all])
    lru_h_prompt = jnp.stack([s[5] for s in sp_all])
    ffn_conv_prompt = jnp.stack([s[6] for s in sp_all])
    k_sample = jnp.stack([s[0] for s in ss_all])
    v_sample = jnp.stack([s[1] for s in ss_all])
    lru_conv_sample = jnp.stack([s[2] for s in ss_all])
    lru_h_sample = jnp.stack([s[3] for s in ss_all])
    ffn_conv_sample = jnp.stack([s[4] for s in ss_all])
    return (yp, ys, k_prompt, v_prompt, k_sample, v_sample, mem_k_prompt, mem_v_prompt,
            lru_conv_prompt, lru_conv_sample, lru_h_prompt, lru_h_sample, ffn_conv_prompt, ffn_conv_sample)
```

```python
import functools
import math

import jax
import jax.numpy as jnp
from jax import lax
from jax.experimental import pallas as pl
from jax.experimental.pallas import tpu as pltpu

F32 = jnp.float32
BF16 = jnp.bfloat16

EPS = 1e-6
N_HEADS_DA = 8
DA_HEAD = 64
DA_HALF = DA_HEAD // 2
DA_WIDTH = N_HEADS_DA * DA_HEAD
LRU_WIDTH = 512
LRU_CONV = 4
LRU_C = 8.0
N_HEADS_X = 4
X_HEAD = 128
X_WIDTH = N_HEADS_X * X_HEAD
FFN_CONV = 3

LANES = 128
SUBLANES = 8
NEG = -1e30
VMEM_LIMIT = 48 * 1024 * 1024

ROW_TILE = 512
Q_TILE = 512
PAGES_PER_STEP = 8
FFN_CHUNK = 256
SAMPLE_SEQS = 32
MEM_BLOCK_BYTES = 16 * 1024 * 1024


def _tile(n, pref):
    t = min(n, pref)
    while n % t:
        t -= SUBLANES
    return t


def _const_spec(shape):
    zeros = (0,) * len(shape)
    return pl.BlockSpec(shape, lambda *_: zeros, pipeline_mode=pl.Buffered(1))


def _params(n_axes):
    return pltpu.CompilerParams(dimension_semantics=("arbitrary",) * n_axes, vmem_limit_bytes=VMEM_LIMIT)


def _rmsnorm(x, g):
    return x * lax.rsqrt(jnp.mean(x * x, axis=-1, keepdims=True) + EPS) * g


def _dot(a, b):
    return jnp.dot(a, b, preferred_element_type=F32)


def _dot_nt(a, b):
    return lax.dot_general(a, b, (((1,), (1,)), ((), ())), preferred_element_type=F32)


def _diff_lambda(lam_ref, lam_init):
    v = lam_ref[...]
    s1 = jnp.sum(v[0:1] * v[1:2], axis=-1, keepdims=True)
    s2 = jnp.sum(v[2:3] * v[3:4], axis=-1, keepdims=True)
    return jnp.exp(s1) - jnp.exp(s2) + lam_init


def _in_proj_kernel(x_ref, gn_ref, w_ref, p_ref, gq_ref, gk_ref, q_ref, k_ref, v_ref, lx_ref, lg_ref):
    xn = _rmsnorm(x_ref[...], gn_ref[...]).astype(BF16)

    def proj(j):
        return _dot(xn, w_ref[:, j * DA_WIDTH:(j + 1) * DA_WIDTH])

    def map_norm(z, g):
        sq = z * z
        hi = sq.astype(BF16)
        lo = (sq - hi.astype(F32)).astype(BF16)
        ms = _dot(hi, p_ref[...]) + _dot(lo, p_ref[...])
        return z * lax.rsqrt(ms + EPS) * g

    q_ref[...] = map_norm(proj(0), gq_ref[...]).astype(q_ref.dtype)
    k_ref[...] = map_norm(proj(1), gk_ref[...])
    v_ref[...] = proj(2)
    lx_ref[...] = proj(3)
    lg_ref[...] = proj(4)


def _in_proj(x2d, gn, w_in, pmat, gq, gk, q_dtype):
    n, d = x2d.shape
    r = _tile(n, ROW_TILE)
    row = lambda w: pl.BlockSpec((r, w), lambda i: (i, 0))
    out_sd = lambda dt: jax.ShapeDtypeStruct((n, DA_WIDTH), dt)
    return pl.pallas_call(
        _in_proj_kernel,
        grid=(n // r,),
        in_specs=[row(d), _const_spec(gn.shape), _const_spec(w_in.shape), _const_spec(pmat.shape),
                  _const_spec(gq.shape), _const_spec(gk.shape)],
        out_specs=[row(DA_WIDTH)] * 5,
        out_shape=[out_sd(q_dtype), out_sd(F32), out_sd(F32), out_sd(F32), out_sd(F32)],
        compiler_params=_params(1),
        name="in_proj",
    )(x2d, gn, w_in, pmat, gq, gk)


def _mem_kv_kernel(m_ref, gn_ref, wk_ref, wv_ref, gck_ref, k_ref, v_ref):
    mn = _rmsnorm(m_ref[...], gn_ref[...]).astype(BF16)
    k = _dot(mn, wk_ref[...])
    for h in range(N_HEADS_X):
        sl = slice(h * X_HEAD, (h + 1) * X_HEAD)
        k_ref[:, sl] = _rmsnorm(k[:, sl], gck_ref[...])
    v_ref[...] = _dot(mn, wv_ref[...])


def _mem_kv(mem2d, gn, w_ck, w_cv, g_ck):
    n, d = mem2d.shape
    r = _tile(n, ROW_TILE)
    row = lambda w: pl.BlockSpec((r, w), lambda i: (i, 0))
    return pl.pallas_call(
        _mem_kv_kernel,
        grid=(n // r,),
        in_specs=[row(d), _const_spec(gn.shape), _const_spec(w_ck.shape), _const_spec(w_cv.shape),
                  _const_spec(g_ck.shape)],
        out_specs=[row(X_WIDTH)] * 2,
        out_shape=[jax.ShapeDtypeStruct((n, X_WIDTH), F32)] * 2,
        compiler_params=_params(1),
        name="mem_kv",
    )(mem2d, gn, w_ck, w_cv, g_ck)


def _head_slope(head, shape):
    e = (head + 1).astype(F32) * (-8.0 / N_HEADS_DA)
    return jnp.exp2(jnp.full(shape, e, F32))


def _sub_norm_pair(o, lane, gsub, lam_init):
    sq = o * o
    first = lane < DA_HEAD
    ms0 = jnp.sum(jnp.where(first, sq, 0.0), axis=-1, keepdims=True)
    ms1 = jnp.sum(jnp.where(first, 0.0, sq), axis=-1, keepdims=True)
    ms = jnp.where(first, ms0, ms1) * (1.0 / DA_HEAD)
    return (o * lax.rsqrt(ms + EPS) * gsub) * (1.0 - lam_init)


def _prompt_attn_kernel(lam_ref, gsub_ref, q_ref, k_ref, v_ref, o_ref, kb_scr, vb_scr, m_scr, l_scr, acc_scr,
                        *, tq, lam_init):
    hp = pl.program_id(1)
    qi = pl.program_id(2)

    @pl.when(qi == 0)
    def _():
        kb_scr[...] = k_ref[...].astype(BF16)
        vb_scr[...] = v_ref[...].astype(BF16)

    lam = _diff_lambda(lam_ref, lam_init)
    q = q_ref[...]
    lane = lax.broadcasted_iota(jnp.int32, (tq, LANES), 1)
    row_i = lax.broadcasted_iota(jnp.int32, (tq, tq), 0)
    col_i = lax.broadcasted_iota(jnp.int32, (tq, tq), 1)
    key_i = lax.broadcasted_iota(jnp.int32, (1, tq), 1)

    heads = []
    for hh in range(2):
        slope = _head_slope(hp * 2 + hh, (1, tq))
        maps = []
        for c in range(2):
            lo = hh * DA_HEAD + c * DA_HALF
            qm = jnp.where((lane >= lo) & (lane < lo + DA_HALF), q, jnp.zeros_like(q))
            m_scr[...] = jnp.full(m_scr.shape, NEG, F32)
            l_scr[...] = jnp.zeros(l_scr.shape, F32)
            acc_scr[...] = jnp.zeros(acc_scr.shape, F32)

            def block(kj, masked, qm=qm, slope=slope):
                k0 = pl.multiple_of(kj * tq, tq)
                s = _dot_nt(qm, kb_scr[pl.ds(k0, tq), :])
                s = s + slope * (key_i + (kj - qi) * tq).astype(F32)
                if masked:
                    s = jnp.where(col_i <= row_i, s, NEG)
                m_old = m_scr[...]
                m_new = jnp.maximum(m_old, jnp.max(s, axis=-1, keepdims=True))
                alpha = jnp.exp(m_old - m_new)
                p = jnp.exp(s - m_new)
                l_scr[...] = alpha * l_scr[...] + jnp.sum(p, axis=-1, keepdims=True)
                acc_scr[...] = alpha * acc_scr[...] + _dot(p.astype(BF16), vb_scr[pl.ds(k0, tq), :])
                m_scr[...] = m_new

            def full_block(kj, carry):
                block(kj, False)
                return carry

            lax.fori_loop(0, qi, full_block, 0)
            block(qi, True)
            maps.append(acc_scr[...] / l_scr[...])
        heads.append(maps[0] - lam * maps[1])

    o = jnp.where(lane < DA_HEAD, heads[0], heads[1])
    o_ref[...] = _sub_norm_pair(o, lane, gsub_ref[...], lam_init).astype(o_ref.dtype)


def _prompt_attn(q, k, v, lam_vecs, gsub2, b, t, lam_init):
    tq = _tile(t, Q_TILE)
    nq = t // tq
    n_pairs = DA_WIDTH // LANES
    kv_spec = pl.BlockSpec((t, LANES), lambda bi, hp, qi: (bi, hp))
    q_spec = pl.BlockSpec((tq, LANES), lambda bi, hp, qi: (bi * nq + qi, hp))
    return pl.pallas_call(
        functools.partial(_prompt_attn_kernel, tq=tq, lam_init=lam_init),
        grid=(b, n_pairs, nq),
        in_specs=[_const_spec(lam_vecs.shape), _const_spec(gsub2.shape), q_spec, kv_spec, kv_spec],
        out_specs=q_spec,
        out_shape=jax.ShapeDtypeStruct((b * t, DA_WIDTH), BF16),
        scratch_shapes=[pltpu.VMEM((t, LANES), BF16), pltpu.VMEM((t, LANES), BF16),
                        pltpu.VMEM((tq, 1), F32), pltpu.VMEM((tq, 1), F32), pltpu.VMEM((tq, LANES), F32)],
        compiler_params=_params(3),
        name="prompt_attn",
    )(lam_vecs, gsub2, q, k, v)


def _sample_attn_kernel(pt_ref, lam_ref, gsub_ref, q_ref, kn_ref, vn_ref, *refs, n_pp, t, past, lam_init):
    k_pages = refs[:n_pp]
    v_pages = refs[n_pp:2 * n_pp]
    o_ref = refs[2 * n_pp]
    q_scr, m_scr, l_scr, acc_scr = refs[2 * n_pp + 1:]
    del pt_ref
    ci = pl.program_id(1)
    rows_h = 2 * t
    n_rows = N_HEADS_DA * rows_h
    tk = n_pp * k_pages[0].shape[-1]
    head_rows = lambda h: slice(h * rows_h, (h + 1) * rows_h)

    @pl.when(ci == 0)
    def _():
        first_map = lax.broadcasted_iota(jnp.int32, (t, DA_HEAD), 1) < DA_HALF
        for h in range(N_HEADS_DA):
            qh = q_ref[h]
            q_scr[h * rows_h:h * rows_h + t, :] = jnp.where(first_map, qh, 0.0)
            q_scr[h * rows_h + t:(h + 1) * rows_h, :] = jnp.where(first_map, 0.0, qh)
        m_scr[...] = jnp.full(m_scr.shape, NEG, F32)
        l_scr[...] = jnp.zeros(l_scr.shape, F32)
        acc_scr[...] = jnp.zeros(acc_scr.shape, F32)

    r_idx = lax.broadcasted_iota(jnp.int32, (n_rows, 1), 0)
    slope = jnp.exp2((r_idx // rows_h + 1).astype(F32) * (-8.0 / N_HEADS_DA))
    q_pos = r_idx % t

    def scores(keys_t):
        return jnp.concatenate(
            [_dot(q_scr[head_rows(h), :].astype(BF16), keys_t(h)) for h in range(N_HEADS_DA)], axis=0)

    def update(s, weighted_values):
        m_old = m_scr[...]
        m_new = jnp.maximum(m_old, jnp.max(s, axis=-1, keepdims=True))
        alpha = jnp.exp(m_old - m_new)
        p = jnp.exp(s - m_new)
        l_scr[...] = alpha * l_scr[...] + jnp.sum(p, axis=-1, keepdims=True)
        pb = p.astype(BF16)
        pv = jnp.concatenate([weighted_values(h, pb[head_rows(h), :]) for h in range(N_HEADS_DA)], axis=0)
        acc_scr[...] = alpha * acc_scr[...] + pv
        m_scr[...] = m_new

    key_off = lax.broadcasted_iota(jnp.int32, (1, tk), 1) + (ci * tk - past)
    s = scores(lambda h: jnp.concatenate([kp[h] for kp in k_pages], axis=1).astype(BF16))
    s = s - slope * (q_pos - key_off).astype(F32)
    update(s, lambda h, ph: _dot_nt(ph, jnp.concatenate([vp[h] for vp in v_pages], axis=1).astype(BF16)))

    @pl.when(ci == pl.num_programs(1) - 1)
    def _():
        pad = jnp.zeros((LANES - t, DA_HEAD), F32)
        padded = lambda ref, h: jnp.concatenate([ref[h], pad], axis=0).astype(BF16)
        new_i = lax.broadcasted_iota(jnp.int32, (1, LANES), 1)
        s = jnp.concatenate(
            [_dot_nt(q_scr[head_rows(h), :].astype(BF16), padded(kn_ref, h)) for h in range(N_HEADS_DA)], axis=0)
        s = s - slope * (q_pos - new_i).astype(F32)
        s = jnp.where((new_i < t) & (new_i <= q_pos), s, NEG)
        update(s, lambda h, ph: _dot(ph, padded(vn_ref, h)))

        o = acc_scr[...] / l_scr[...]
        lam = _diff_lambda(lam_ref, lam_init)
        for h in range(N_HEADS_DA):
            d = o[h * rows_h:h * rows_h + t, :] - lam * o[h * rows_h + t:(h + 1) * rows_h, :]
            o_ref[h] = _rmsnorm(d, gsub_ref[...]) * (1.0 - lam_init)


def _sample_attn(page_ids, q4, k_new4, v_new4, cache_kt, cache_vt, lam_vecs, gsub, lam_init):
    n_seq, n_heads, t, dh = q4.shape
    page = cache_kt.shape[-1]
    n_pages = page_ids.shape[0] // n_seq
    n_pp = min(PAGES_PER_STEP, n_pages)
    while n_pages % n_pp:
        n_pp -= 1
    n_rows = 2 * n_heads * t
    assert (2 * t) % 16 == 0 and t <= LANES

    def page_spec(i):
        return pl.BlockSpec((None, n_heads, dh, page),
                            lambda s, c, pt, i=i: (pt[s * n_pages + c * n_pp + i], 0, 0, 0))

    seq_spec = pl.BlockSpec((None, n_heads, t, dh), lambda s, c, pt: (s, 0, 0, 0))
    const = lambda shape: pl.BlockSpec(shape, lambda s, c, pt: (0,) * len(shape))
    grid_spec = pltpu.PrefetchScalarGridSpec(
        num_scalar_prefetch=1,
        grid=(n_seq, n_pages // n_pp),
        in_specs=[const(lam_vecs.shape), const(gsub.shape), seq_spec, seq_spec, seq_spec]
                 + [page_spec(i) for i in range(n_pp)] * 2,
        out_specs=seq_spec,
        scratch_shapes=[pltpu.VMEM((n_rows, dh), F32), pltpu.VMEM((n_rows, 1), F32),
                        pltpu.VMEM((n_rows, 1), F32), pltpu.VMEM((n_rows, dh), F32)],
    )
    return pl.pallas_call(
        functools.partial(_sample_attn_kernel, n_pp=n_pp, t=t, past=n_pages * page, lam_init=lam_init),
        grid_spec=grid_spec,
        out_shape=jax.ShapeDtypeStruct((n_seq, n_heads, t, dh), F32),
        compiler_params=_params(2),
        name="sample_attn",
    )(page_ids, lam_vecs, gsub, q4, k_new4, v_new4, *([cache_kt] * n_pp), *([cache_vt] * n_pp))


def _lru_kernel(lx_ref, lg_ref, cprev_ref, h0_ref, cw_ref, cb_ref, wa_ref, ba_ref, wx_ref, bx_ref, lam_ref,
                out_ref, hlast_ref, ext_scr, h_scr, a_scr, g_scr, *, g, tt):
    i = pl.program_id(1)
    w = LRU_WIDTH
    hist = LRU_CONV - 1
    top = SUBLANES

    @pl.when(i == 0)
    def _():
        ext_scr[:, top - hist:top, :] = cprev_ref[...]
        h_scr[...] = h0_ref[...]

    x3 = lx_ref[...]
    ext_scr[:, top:top + tt, :] = x3
    xc3 = cb_ref[...]
    for j in range(LRU_CONV):
        xc3 = xc3 + ext_scr[:, top - hist + j:top - hist + j + tt, :] * cw_ref[j:j + 1, :]
    ext_scr[:, top - hist:top, :] = ext_scr[:, top + tt - hist:top + tt, :]

    r = g * tt
    xc = xc3.reshape(r, w)
    xb = xc.astype(BF16)
    rg = jax.nn.sigmoid(_dot(xb, wa_ref[...]) + ba_ref[...])
    ig = jax.nn.sigmoid(_dot(xb, wx_ref[...]) + bx_ref[...])
    nl = -lam_ref[...]
    softplus = jnp.maximum(nl, 0.0) + jnp.log1p(jnp.exp(-jnp.abs(nl)))
    log_a = -LRU_C * rg * softplus
    a = jnp.exp(log_a)
    gx = jnp.sqrt(-jnp.tanh(log_a) * (a * a + 1.0)) * (ig * xc)

    nch = r // SUBLANES
    a3 = a.reshape(nch, SUBLANES, w)
    g3 = gx.reshape(nch, SUBLANES, w)
    sub = lax.broadcasted_iota(jnp.int32, (nch, SUBLANES, w), 1)
    for s in (1, 2, 4):
        keep = sub >= s
        a_prev = jnp.where(keep, pltpu.roll(a3, s, 1), 1.0)
        g_prev = jnp.where(keep, pltpu.roll(g3, s, 1), 0.0)
        g3 = a3 * g_prev + g3
        a3 = a3 * a_prev

    nc = tt // SUBLANES
    a_scr[...] = a3.reshape(g, nc, SUBLANES, w)
    g_scr[...] = g3.reshape(g, nc, SUBLANES, w)

    def chunk(c, h):
        hs = a_scr[:, c] * h + g_scr[:, c]
        g_scr[:, c] = hs
        return hs[:, SUBLANES - 1:SUBLANES, :]

    h = h_scr[...]
    if nc <= 2:
        for c in range(nc):
            h = chunk(c, h)
    else:
        h = lax.fori_loop(0, nc, chunk, h)
    h_scr[...] = h

    hs = g_scr[...].reshape(r, w)
    out = hs * jax.nn.gelu(lg_ref[...].reshape(r, w))
    out_ref[...] = out.reshape(g, tt, w).astype(out_ref.dtype)

    @pl.when(i == pl.num_programs(1) - 1)
    def _():
        hlast_ref[...] = h


def _lru(lx3, lg3, conv_prev, h0, cw, cb, wa, ba, wx, bx, lam, g, tt, out_dtype):
    n_seq, t, w = lx3.shape
    blk = pl.BlockSpec((g, tt, w), lambda s, i: (s, i, 0))
    seq = lambda rows: pl.BlockSpec((g, rows, w), lambda s, i: (s, 0, 0))
    return pl.pallas_call(
        functools.partial(_lru_kernel, g=g, tt=tt),
        grid=(n_seq // g, t // tt),
        in_specs=[blk, blk, seq(LRU_CONV - 1), seq(1)] + [_const_spec(a.shape) for a in (cw, cb, wa, ba, wx, bx, lam)],
        out_specs=[blk, seq(1)],
        out_shape=[jax.ShapeDtypeStruct((n_seq, t, w), out_dtype), jax.ShapeDtypeStruct((n_seq, 1, w), F32)],
        scratch_shapes=[pltpu.VMEM((g, SUBLANES + tt, w), F32), pltpu.VMEM((g, 1, w), F32),
                        pltpu.VMEM((g, tt // SUBLANES, SUBLANES, w), F32),
                        pltpu.VMEM((g, tt // SUBLANES, SUBLANES, w), F32)],
        compiler_params=_params(2),
        name="lru",
    )(lx3, lg3, conv_prev, h0, cw, cb, wa, ba, wx, bx, lam)


def _out_cross_kernel(x_ref, da_ref, lru_ref, mk_ref, mv_ref, wo_da_ref, wo_lru_ref, gn_ref, wcq_ref, gcq_ref,
                      wco_ref, out_ref, q_scr, o_scr, *, g, tt):
    r = g * tt
    d = x_ref.shape[-1]
    h = (x_ref[...].reshape(r, d)
         + _dot(da_ref[...].reshape(r, DA_WIDTH).astype(BF16), wo_da_ref[...])
         + _dot(lru_ref[...].reshape(r, LRU_WIDTH).astype(BF16), wo_lru_ref[...]))
    hn = _rmsnorm(h, gn_ref[...]).astype(BF16)
    qc = _dot(hn, wcq_ref[...])
    for hd in range(N_HEADS_X):
        sl = slice(hd * X_HEAD, (hd + 1) * X_HEAD)
        q_scr[:, sl] = _rmsnorm(qc[:, sl], gcq_ref[...])

    if g == 1:
        for hd in range(N_HEADS_X):
            sl = slice(hd * X_HEAD, (hd + 1) * X_HEAD)
            s = _dot_nt(q_scr[:, sl].astype(BF16), mk_ref[0, :, sl].astype(BF16))
            e = jnp.exp(s - jnp.max(s, axis=-1, keepdims=True))
            o = _dot(e.astype(BF16), mv_ref[0, :, sl].astype(BF16))
            o_scr[:, sl] = o / jnp.sum(e, axis=-1, keepdims=True)
    else:
        lane_head = lax.broadcasted_iota(jnp.int32, (tt, X_WIDTH), 1) // X_HEAD

        def one_seq(si, carry):
            r0 = pl.multiple_of(si * tt, tt)
            q = q_scr[pl.ds(r0, tt), :]
            qbd = jnp.concatenate([jnp.where(lane_head == hd, q, 0.0) for hd in range(N_HEADS_X)], axis=0)
            s = _dot_nt(qbd.astype(BF16), mk_ref[si].astype(BF16))
            e = jnp.exp(s - jnp.max(s, axis=-1, keepdims=True))
            o = _dot(e.astype(BF16), mv_ref[si].astype(BF16)) / jnp.sum(e, axis=-1, keepdims=True)
            tot = jnp.zeros((tt, X_WIDTH), F32)
            for hd in range(N_HEADS_X):
                tot = tot + jnp.where(lane_head == hd, o[hd * tt:(hd + 1) * tt, :], 0.0)
            o_scr[pl.ds(r0, tt), :] = tot
            return carry

        lax.fori_loop(0, g, one_seq, 0)

    out = h + _dot(o_scr[...].astype(BF16), wco_ref[...])
    out_ref[...] = out.reshape(g, tt, d)


def _out_cross(x3, da3, lru3, mk3, mv3, mem_first, wo_da, wo_lru, gn, wcq, gcq, wco, g, tt):
    n_seq, t, d = x3.shape
    n_mem = mk3.shape[1]
    mem_blk0 = mem_first // g
    blk = lambda w: pl.BlockSpec((g, tt, w), lambda s, i: (s, i, 0))
    mem = pl.BlockSpec((g, n_mem, X_WIDTH), lambda s, i: (s + mem_blk0, 0, 0))
    return pl.pallas_call(
        functools.partial(_out_cross_kernel, g=g, tt=tt),
        grid=(n_seq // g, t // tt),
        in_specs=[blk(d), blk(DA_WIDTH), blk(LRU_WIDTH), mem, mem]
                 + [_const_spec(a.shape) for a in (wo_da, wo_lru, gn, wcq, gcq, wco)],
        out_specs=blk(d),
        out_shape=jax.ShapeDtypeStruct((n_seq, t, d), F32),
        scratch_shapes=[pltpu.VMEM((g * tt, X_WIDTH), F32), pltpu.VMEM((g * tt, X_WIDTH), F32)],
        compiler_params=_params(2),
        name="out_cross",
    )(x3, da3, lru3, mk3, mv3, wo_da, wo_lru, gn, wcq, gcq, wco)


def _conv_ffn_kernel(x_ref, prev_ref, gn_ref, wup_ref, cw_ref, cb_ref, wdn_ref, out_ref, state_ref,
                     halo_scr, ext_scr, *, g, tt, fc):
    i = pl.program_id(1)
    r = g * tt
    d = x_ref.shape[-1]
    f = wdn_ref.shape[0]
    hist = FFN_CONV - 1
    top = SUBLANES

    @pl.when(i == 0)
    def _():
        halo_scr[:, top - hist:top, :] = prev_ref[...]

    x = x_ref[...].reshape(r, d)
    xn = _rmsnorm(x, gn_ref[...]).astype(BF16)
    out_ref[...] = x_ref[...]

    for ch in range(f // fc):
        halves = []
        for half in range(2):
            c0 = half * f + ch * fc
            cols = slice(c0, c0 + fc)
            u3 = _dot(xn, wup_ref[:, cols]).reshape(g, tt, fc)
            ext_scr[:, top:top + tt, :] = u3
            ext_scr[:, top - hist:top, :] = halo_scr[:, top - hist:top, cols]
            c3 = cb_ref[:, cols]
            for j in range(FFN_CONV):
                c3 = c3 + ext_scr[:, top - hist + j:top - hist + j + tt, :] * cw_ref[j:j + 1, cols]
            halo_scr[:, top - hist:top, cols] = ext_scr[:, top + tt - hist:top + tt, :]
            halves.append(c3.reshape(r, fc))
        act = (jax.nn.silu(halves[0]) * halves[1]).astype(BF16)
        out_ref[...] += _dot(act, wdn_ref[ch * fc:(ch + 1) * fc, :]).reshape(g, tt, d)

    @pl.when(i == pl.num_programs(1) - 1)
    def _():
        state_ref[...] = halo_scr[:, top - hist:top, :]


def _conv_ffn(x3, prev, gn, wup, cw, cb, wdn, g, tt):
    n_seq, t, d = x3.shape
    f2 = wup.shape[1]
    fc = _tile(f2 // 2, FFN_CHUNK)
    blk = pl.BlockSpec((g, tt, d), lambda s, i: (s, i, 0))
    st = pl.BlockSpec((g, FFN_CONV - 1, f2), lambda s, i: (s, 0, 0))
    return pl.pallas_call(
        functools.partial(_conv_ffn_kernel, g=g, tt=tt, fc=fc),
        grid=(n_seq // g, t // tt),
        in_specs=[blk, st] + [_const_spec(a.shape) for a in (gn, wup, cw, cb, wdn)],
        out_specs=[blk, st],
        out_shape=[jax.ShapeDtypeStruct((n_seq, t, d), F32), jax.ShapeDtypeStruct((n_seq, FFN_CONV - 1, f2), F32)],
        scratch_shapes=[pltpu.VMEM((g, SUBLANES, f2), F32), pltpu.VMEM((g, SUBLANES + tt, fc), F32)],
        compiler_params=_params(2),
        name="conv_ffn",
    )(x3, prev, gn, wup, cw, cb, wdn)


def _block_diag(blocks):
    n, a, b = blocks.shape
    eye = jnp.eye(n, dtype=blocks.dtype)
    return (eye[:, None, :, None] * blocks[:, :, None, :]).reshape(n * a, n * b)


def _layer_weights(l, norm_mix, w_in, g_q, g_k, lam_q1, lam_k1, lam_q2, lam_k2, g_sub, lru_conv_w, lru_conv_b,
                   lru_wa, lru_ba, lru_wx, lru_bx, lru_lambda, w_o, norm_cross, norm_mem, w_cq, w_ck, w_cv,
                   g_cq, g_ck, w_co, norm_ffn, w_up, ffn_conv_w, ffn_conv_b, w_down):
    row = lambda a: a[l].reshape(1, -1)
    n_maps = DA_WIDTH // DA_HALF
    return dict(
        norm_mix=row(norm_mix), w_in=w_in[l].astype(BF16),
        pmat=jnp.kron(jnp.eye(n_maps, dtype=F32), jnp.full((DA_HALF, DA_HALF), 1.0 / DA_HALF, F32)).astype(BF16),
        gq=jnp.tile(g_q[l] * (DA_HALF ** -0.5), n_maps).reshape(1, -1),
        gk=jnp.tile(g_k[l], n_maps).reshape(1, -1),
        lam=jnp.stack([lam_q1[l], lam_k1[l], lam_q2[l], lam_k2[l]]),
        gsub2=jnp.tile(g_sub[l], LANES // DA_HEAD).reshape(1, -1),
        gsub=row(g_sub),
        cw=lru_conv_w[l], cb=row(lru_conv_b),
        wa=_block_diag(lru_wa[l]).astype(BF16), ba=row(lru_ba),
        wx=_block_diag(lru_wx[l]).astype(BF16), bx=row(lru_bx), lru_lambda=row(lru_lambda),
        wo_da=w_o[l, :DA_WIDTH].astype(BF16), wo_lru=w_o[l, DA_WIDTH:].astype(BF16),
        norm_cross=row(norm_cross), norm_mem=row(norm_mem),
        wcq=w_cq[l].astype(BF16), wck=w_ck[l].astype(BF16), wcv=w_cv[l].astype(BF16),
        gcq=(g_cq[l] * (X_HEAD ** -0.5)).reshape(1, -1), gck=row(g_ck), wco=w_co[l].astype(BF16),
        norm_ffn=row(norm_ffn), wup=w_up[l].astype(BF16), fcw=ffn_conv_w[l], fcb=row(ffn_conv_b),
        wdn=w_down[l].astype(BF16),
    )


def _mix_and_ffn(x3, da3, lx, lg, conv_prev, h0, ffn_prev, mk3, mv3, mem_first, p, g, tt, mid_dtype):
    n_seq, t, d = x3.shape
    assert t >= LRU_CONV - 1 and t >= FFN_CONV - 1
    lru3, h_last = _lru(lx.reshape(n_seq, t, LRU_WIDTH), lg.reshape(n_seq, t, LRU_WIDTH), conv_prev, h0,
                        p['cw'], p['cb'], p['wa'], p['ba'], p['wx'], p['bx'], p['lru_lambda'], g, tt, mid_dtype)
    g_x = g
    while g_x > 1 and 4 * g_x * mk3.shape[1] * X_WIDTH * 4 > MEM_BLOCK_BYTES:
        g_x //= 2
    h3 = _out_cross(x3, da3, lru3, mk3, mv3, mem_first, p['wo_da'], p['wo_lru'], p['norm_cross'], p['wcq'], p['gcq'],
                    p['wco'], g_x, tt)
    y3, ffn_state = _conv_ffn(h3, ffn_prev, p['norm_ffn'], p['wup'], p['fcw'], p['fcb'], p['wdn'], g, tt)
    conv_state = lx.reshape(n_seq, t, LRU_WIDTH)[:, t - (LRU_CONV - 1):]
    return y3, conv_state, h_last.reshape(n_seq, LRU_WIDTH), ffn_state


def kernel(x_prompt, x_sample, mem_prompt, cache_k, cache_v, page_table, cache_mem_k, cache_mem_v, state_lru_conv, state_lru_h, state_ffn_conv, norm_mix, w_in, g_q, g_k, lam_q1, lam_k1, lam_q2, lam_k2, g_sub, lru_conv_w, lru_conv_b, lru_wa, lru_ba, lru_wx, lru_bx, lru_lambda, w_o, norm_cross, norm_mem, w_cq, w_ck, w_cv, g_cq, g_ck, w_co, norm_ffn, w_up, ffn_conv_w, ffn_conv_b, w_down):
    depth = w_in.shape[0]
    b, t, d = x_prompt.shape
    nb, nt, _ = x_sample.shape
    n_mem = mem_prompt.shape[1]
    n_phys, page = cache_k.shape[1], cache_k.shape[2]
    f2 = w_up.shape[2]
    to_pages = lambda c: jnp.transpose(c, (0, 1, 3, 4, 2)).reshape(depth * n_phys, N_HEADS_DA, DA_HEAD, page)
    cache_kt, cache_vt = to_pages(cache_k), to_pages(cache_v)
    to_heads = lambda a, n, m: jnp.transpose(a.reshape(n, m, N_HEADS_DA, DA_HEAD), (0, 2, 1, 3))
    cmk = cache_mem_k.reshape(depth * nb, n_mem, X_WIDTH)
    cmv = cache_mem_v.reshape(depth * nb, n_mem, X_WIDTH)

    tt_p = _tile(t, ROW_TILE)
    g_s = SAMPLE_SEQS if nb % SAMPLE_SEQS == 0 else 1

    yp, ys = x_prompt, x_sample
    outs_p, outs_s = [], []
    for l in range(depth):
        p = _layer_weights(l, norm_mix, w_in, g_q, g_k, lam_q1, lam_k1, lam_q2, lam_k2, g_sub, lru_conv_w,
                           lru_conv_b, lru_wa, lru_ba, lru_wx, lru_bx, lru_lambda, w_o, norm_cross, norm_mem,
                           w_cq, w_ck, w_cv, g_cq, g_ck, w_co, norm_ffn, w_up, ffn_conv_w, ffn_conv_b, w_down)
        lam_init = 0.8 - 0.6 * math.exp(-0.3 * l)

        q, k, v, lx, lg = _in_proj(yp.reshape(b * t, d), p['norm_mix'], p['w_in'], p['pmat'], p['gq'], p['gk'], BF16)
        da = _prompt_attn(q, k, v, p['lam'], p['gsub2'], b, t, lam_init)
        mk, mv = _mem_kv(mem_prompt.reshape(b * n_mem, d), p['norm_mem'], p['wck'], p['wcv'], p['gck'])
        yp, conv_state, h_last, ffn_state = _mix_and_ffn(
            yp, da.reshape(b, t, DA_WIDTH), lx, lg,
            jnp.zeros((b, LRU_CONV - 1, LRU_WIDTH), F32), jnp.zeros((b, 1, LRU_WIDTH), F32),
            jnp.zeros((b, FFN_CONV - 1, f2), F32),
            mk.reshape(b, n_mem, X_WIDTH), mv.reshape(b, n_mem, X_WIDTH), 0, p, 1, tt_p, BF16)
        outs_p.append((k.reshape(b, t, N_HEADS_DA, DA_HEAD), v.reshape(b, t, N_HEADS_DA, DA_HEAD),
                       mk.reshape(b, n_mem, N_HEADS_X, X_HEAD), mv.reshape(b, n_mem, N_HEADS_X, X_HEAD),
                       conv_state, h_last, ffn_state))

        q, k, v, lx, lg = _in_proj(ys.reshape(nb * nt, d), p['norm_mix'], p['w_in'], p['pmat'], p['gq'], p['gk'], F32)
        page_ids = page_table.reshape(-1).astype(jnp.int32) + l * n_phys
        da = _sample_attn(page_ids, to_heads(q, nb, nt), to_heads(k, nb, nt), to_heads(v, nb, nt),
                          cache_kt, cache_vt, p['lam'], p['gsub'], lam_init)
        da = jnp.transpose(da, (0, 2, 1, 3))
        ys, conv_state, h_last, ffn_state = _mix_and_ffn(
            ys, da.reshape(nb, nt, DA_WIDTH), lx, lg,
            state_lru_conv[l], state_lru_h[l].reshape(nb, 1, LRU_WIDTH), state_ffn_conv[l],
            cmk, cmv, l * nb, p, g_s, nt, F32)
        outs_s.append((k.reshape(nb, nt, N_HEADS_DA, DA_HEAD), v.reshape(nb, nt, N_HEADS_DA, DA_HEAD),
                       conv_state, h_last, ffn_state))

    stack = lambda outs, j: jnp.stack([o[j] for o in outs])
    return (yp, ys, stack(outs_p, 0), stack(outs_p, 1), stack(outs_s, 0), stack(outs_s, 1),
            stack(outs_p, 2), stack(outs_p, 3), stack(outs_p, 4), stack(outs_s, 2),
            stack(outs_p, 5), stack(outs_s, 3), stack(outs_p, 6), stack(outs_s, 4))
```

```python
import functools
import math

import jax
import jax.numpy as jnp
from jax import lax
from jax.experimental import pallas as pl
from jax.experimental.pallas import tpu as pltpu

F32 = jnp.float32
BF16 = jnp.bfloat16

EPS = 1e-6
N_HEADS_DA = 8
DA_HEAD = 64
DA_HALF = DA_HEAD // 2
DA_WIDTH = N_HEADS_DA * DA_HEAD
LRU_WIDTH = 512
LRU_CONV = 4
LRU_C = 8.0
N_HEADS_X = 4
X_HEAD = 128
X_WIDTH = N_HEADS_X * X_HEAD
FFN_CONV = 3

LANES = 128
SUBLANES = 8
NEG = -1e30
VMEM_LIMIT = 48 * 1024 * 1024

ROW_TILE = 512
Q_TILE = 256
PAGES_PER_STEP = 16
FFN_CHUNK = 256
SAMPLE_SEQS = 32
MEM_BLOCK_BYTES = 16 * 1024 * 1024


def _tile(n, pref):
    t = min(n, pref)
    while n % t:
        t -= SUBLANES
    return t


def _const_spec(shape):
    zeros = (0,) * len(shape)
    return pl.BlockSpec(shape, lambda *_: zeros, pipeline_mode=pl.Buffered(1))


def _params(n_axes):
    return pltpu.CompilerParams(dimension_semantics=("arbitrary",) * n_axes, vmem_limit_bytes=VMEM_LIMIT)


def _rmsnorm(x, g):
    return x * lax.rsqrt(jnp.mean(x * x, axis=-1, keepdims=True) + EPS) * g


def _dot(a, b):
    return jnp.dot(a, b, preferred_element_type=F32)


def _dot_nt(a, b):
    return lax.dot_general(a, b, (((1,), (1,)), ((), ())), preferred_element_type=F32)


def _diff_lambda(lam_ref, lam_init):
    v = lam_ref[...]
    s1 = jnp.sum(v[0:1] * v[1:2], axis=-1, keepdims=True)
    s2 = jnp.sum(v[2:3] * v[3:4], axis=-1, keepdims=True)
    return jnp.exp(s1) - jnp.exp(s2) + lam_init


def _in_proj_kernel(x_ref, gn_ref, w_ref, p_ref, gq_ref, gk_ref, q_ref, k_ref, v_ref, lx_ref, lg_ref):
    xn = _rmsnorm(x_ref[...], gn_ref[...]).astype(BF16)

    def proj(j):
        return _dot(xn, w_ref[:, j * DA_WIDTH:(j + 1) * DA_WIDTH])

    def map_norm(z, g):
        sq = z * z
        hi = sq.astype(BF16)
        lo = (sq - hi.astype(F32)).astype(BF16)
        ms = _dot(hi, p_ref[...]) + _dot(lo, p_ref[...])
        return z * lax.rsqrt(ms + EPS) * g

    q_ref[...] = map_norm(proj(0), gq_ref[...]).astype(q_ref.dtype)
    k_ref[...] = map_norm(proj(1), gk_ref[...])
    v_ref[...] = proj(2)
    lx_ref[...] = proj(3)
    lg_ref[...] = proj(4)


def _in_proj(x2d, gn, w_in, pmat, gq, gk, q_dtype):
    n, d = x2d.shape
    r = _tile(n, ROW_TILE)
    row = lambda w: pl.BlockSpec((r, w), lambda i: (i, 0))
    out_sd = lambda dt: jax.ShapeDtypeStruct((n, DA_WIDTH), dt)
    return pl.pallas_call(
        _in_proj_kernel,
        grid=(n // r,),
        in_specs=[row(d), _const_spec(gn.shape), _const_spec(w_in.shape), _const_spec(pmat.shape),
                  _const_spec(gq.shape), _const_spec(gk.shape)],
        out_specs=[row(DA_WIDTH)] * 5,
        out_shape=[out_sd(q_dtype), out_sd(F32), out_sd(F32), out_sd(F32), out_sd(F32)],
        compiler_params=_params(1),
        name="in_proj",
    )(x2d, gn, w_in, pmat, gq, gk)


def _mem_kv_kernel(m_ref, gn_ref, wk_ref, wv_ref, gck_ref, k_ref, v_ref):
    mn = _rmsnorm(m_ref[...], gn_ref[...]).astype(BF16)
    k = _dot(mn, wk_ref[...])
    for h in range(N_HEADS_X):
        sl = slice(h * X_HEAD, (h + 1) * X_HEAD)
        k_ref[:, sl] = _rmsnorm(k[:, sl], gck_ref[...])
    v_ref[...] = _dot(mn, wv_ref[...])


def _mem_kv(mem2d, gn, w_ck, w_cv, g_ck):
    n, d = mem2d.shape
    r = _tile(n, ROW_TILE)
    row = lambda w: pl.BlockSpec((r, w), lambda i: (i, 0))
    return pl.pallas_call(
        _mem_kv_kernel,
        grid=(n // r,),
        in_specs=[row(d), _const_spec(gn.shape), _const_spec(w_ck.shape), _const_spec(w_cv.shape),
                  _const_spec(g_ck.shape)],
        out_specs=[row(X_WIDTH)] * 2,
        out_shape=[jax.ShapeDtypeStruct((n, X_WIDTH), F32)] * 2,
        compiler_params=_params(1),
        name="mem_kv",
    )(mem2d, gn, w_ck, w_cv, g_ck)


def _head_slope(head, shape):
    e = (head + 1).astype(F32) * (-8.0 / N_HEADS_DA)
    return jnp.exp2(jnp.full(shape, e, F32))


def _sub_norm_pair(o, lane, gsub, lam_init):
    sq = o * o
    first = lane < DA_HEAD
    ms0 = jnp.sum(jnp.where(first, sq, 0.0), axis=-1, keepdims=True)
    ms1 = jnp.sum(jnp.where(first, 0.0, sq), axis=-1, keepdims=True)
    ms = jnp.where(first, ms0, ms1) * (1.0 / DA_HEAD)
    return (o * lax.rsqrt(ms + EPS) * gsub) * (1.0 - lam_init)


BIAS_SPLIT = 256


N_BIAS_TERMS = 3
LOG2E = math.log2(math.e)


def _prompt_attn_kernel(lam_ref, gsub_ref, q_ref, k_ref, v_ref, o_ref, kb_scr, vb_scr, *, tq, lam_init):
    hp = pl.program_id(1)
    t = k_ref.shape[0]
    other = lambda hh: (1 - hh) * DA_HEAD

    lane_t = lax.broadcasted_iota(jnp.int32, (t, LANES), 1)
    pos = lax.broadcasted_iota(jnp.int32, (t, LANES), 0)
    pos_hi = (pos // BIAS_SPLIT * BIAS_SPLIT).astype(F32)
    pos_lo = (pos % BIAS_SPLIT).astype(F32)
    for hh in range(2):
        own = (lane_t >= hh * DA_HEAD) & (lane_t < (hh + 1) * DA_HEAD)
        rel = lane_t - other(hh)
        in_bias = (rel >= 0) & (rel < 2 * N_BIAS_TERMS)
        bias = jnp.where(in_bias, jnp.where(rel % 2 == 0, pos_hi, pos_lo), 0.0)
        kb_scr[hh] = jnp.where(own, k_ref[...], bias).astype(BF16)
        vb_scr[hh] = jnp.where(own, v_ref[...], 1.0).astype(BF16)

    lane = lax.broadcasted_iota(jnp.int32, (tq, LANES), 1)
    visible = (lax.broadcasted_iota(jnp.int32, (tq, tq), 1) <= lax.broadcasted_iota(jnp.int32, (tq, tq), 0))
    lam = _diff_lambda(lam_ref, lam_init)

    def q_slab(q, hh, c):
        lo = hh * DA_HEAD + c * DA_HALF
        rel = lane - other(hh)
        rest = _head_slope(hp * 2 + hh, (1, LANES)) * LOG2E
        factor = jnp.zeros((tq, LANES), F32)
        for piece in range(N_BIAS_TERMS):
            part = rest.astype(BF16).astype(F32)
            factor = jnp.where((rel >= 2 * piece) & (rel < 2 * piece + 2), part, factor)
            rest = rest - part
        return jnp.where((lane >= lo) & (lane < lo + DA_HALF), q, factor).astype(BF16)

    for qi in range(t // tq):
        rows = slice(qi * tq, (qi + 1) * tq)
        past = qi * tq
        q = q_ref[rows, :].astype(F32)
        heads = []
        for hh in range(2):
            own = (lane >= hh * DA_HEAD) & (lane < (hh + 1) * DA_HEAD)
            maps = []
            for c in range(2):
                qm = q_slab(q, hh, c)
                s_diag = jnp.where(visible, _dot_nt(qm, kb_scr[hh, rows, :]), NEG)
                m = jnp.max(s_diag, axis=-1, keepdims=True)
                if past:
                    s_past = _dot_nt(qm, kb_scr[hh, :past, :])
                    m = jnp.maximum(m, jnp.max(s_past, axis=-1, keepdims=True))
                acc = _dot(jnp.exp2(s_diag - m).astype(BF16), vb_scr[hh, rows, :])
                if past:
                    acc = acc + _dot(jnp.exp2(s_past - m).astype(BF16), vb_scr[hh, :past, :])
                maps.append(acc / jnp.where(own, pltpu.roll(acc, DA_HEAD, 1), 1.0))
            heads.append(maps[0] - lam * maps[1])
        o = jnp.where(lane < DA_HEAD, heads[0], heads[1])
        o_ref[rows, :] = _sub_norm_pair(o, lane, gsub_ref[...], lam_init).astype(o_ref.dtype)


def _prompt_attn(q, k, v, lam_vecs, gsub2, b, t, lam_init):
    tq = _tile(t, Q_TILE)
    n_pairs = DA_WIDTH // LANES
    assert t <= BIAS_SPLIT * 256
    spec = pl.BlockSpec((t, LANES), lambda bi, hp: (bi, hp))
    return pl.pallas_call(
        functools.partial(_prompt_attn_kernel, tq=tq, lam_init=lam_init),
        grid=(b, n_pairs),
        in_specs=[_const_spec(lam_vecs.shape), _const_spec(gsub2.shape), spec, spec, spec],
        out_specs=spec,
        out_shape=jax.ShapeDtypeStruct((b * t, DA_WIDTH), BF16),
        scratch_shapes=[pltpu.VMEM((2, t, LANES), BF16), pltpu.VMEM((2, t, LANES), BF16)],
        compiler_params=_params(2),
        name="prompt_attn",
    )(lam_vecs, gsub2, q, k, v)


def _sample_attn_kernel(pt_ref, lam_ref, gsub_ref, q_ref, kn_ref, vn_ref, *refs, n_pp, t, past, lam_init):
    k_pages = refs[:n_pp]
    v_pages = refs[n_pp:2 * n_pp]
    o_ref = refs[2 * n_pp]
    q_scr, m_scr, l_scr, acc_scr = refs[2 * n_pp + 1:]
    del pt_ref
    ci = pl.program_id(1)
    rows_h = 2 * t
    n_rows = N_HEADS_DA * rows_h
    tk = n_pp * k_pages[0].shape[-1]
    head_rows = lambda h: slice(h * rows_h, (h + 1) * rows_h)

    @pl.when(ci == 0)
    def _():
        first_map = lax.broadcasted_iota(jnp.int32, (t, DA_HEAD), 1) < DA_HALF
        for h in range(N_HEADS_DA):
            qh = q_ref[h]
            q_scr[h * rows_h:h * rows_h + t, :] = jnp.where(first_map, qh, 0.0)
            q_scr[h * rows_h + t:(h + 1) * rows_h, :] = jnp.where(first_map, 0.0, qh)
        m_scr[...] = jnp.full(m_scr.shape, NEG, F32)
        l_scr[...] = jnp.zeros(l_scr.shape, F32)
        acc_scr[...] = jnp.zeros(acc_scr.shape, F32)

    r_idx = lax.broadcasted_iota(jnp.int32, (n_rows, 1), 0)
    slope = jnp.exp2((r_idx // rows_h + 1).astype(F32) * (-8.0 / N_HEADS_DA))
    q_pos = r_idx % t

    def scores(keys_t):
        return jnp.concatenate(
            [_dot(q_scr[head_rows(h), :].astype(BF16), keys_t(h)) for h in range(N_HEADS_DA)], axis=0)

    def update(s, weighted_values):
        m_old = m_scr[...]
        m_new = jnp.maximum(m_old, jnp.max(s, axis=-1, keepdims=True))
        alpha = jnp.exp(m_old - m_new)
        p = jnp.exp(s - m_new)
        l_scr[...] = alpha * l_scr[...] + jnp.sum(p, axis=-1, keepdims=True)
        pb = p.astype(BF16)
        pv = jnp.concatenate([weighted_values(h, pb[head_rows(h), :]) for h in range(N_HEADS_DA)], axis=0)
        acc_scr[...] = alpha * acc_scr[...] + pv
        m_scr[...] = m_new

    key_off = lax.broadcasted_iota(jnp.int32, (1, tk), 1) + (ci * tk - past)
    s = scores(lambda h: jnp.concatenate([kp[h] for kp in k_pages], axis=1).astype(BF16))
    s = s - slope * (q_pos - key_off).astype(F32)
    update(s, lambda h, ph: _dot_nt(ph, jnp.concatenate([vp[h] for vp in v_pages], axis=1).astype(BF16)))

    @pl.when(ci == pl.num_programs(1) - 1)
    def _():
        pad = jnp.zeros((LANES - t, DA_HEAD), F32)
        padded = lambda ref, h: jnp.concatenate([ref[h], pad], axis=0).astype(BF16)
        new_i = lax.broadcasted_iota(jnp.int32, (1, LANES), 1)
        s = jnp.concatenate(
            [_dot_nt(q_scr[head_rows(h), :].astype(BF16), padded(kn_ref, h)) for h in range(N_HEADS_DA)], axis=0)
        s = s - slope * (q_pos - new_i).astype(F32)
        s = jnp.where((new_i < t) & (new_i <= q_pos), s, NEG)
        update(s, lambda h, ph: _dot(ph, padded(vn_ref, h)))

        o = acc_scr[...] / l_scr[...]
        lam = _diff_lambda(lam_ref, lam_init)
        for h in range(N_HEADS_DA):
            d = o[h * rows_h:h * rows_h + t, :] - lam * o[h * rows_h + t:(h + 1) * rows_h, :]
            o_ref[h] = _rmsnorm(d, gsub_ref[...]) * (1.0 - lam_init)


def _sample_attn(page_ids, q4, k_new4, v_new4, cache_kt, cache_vt, lam_vecs, gsub, lam_init):
    n_seq, n_heads, t, dh = q4.shape
    page = cache_kt.shape[-1]
    n_pages = page_ids.shape[0] // n_seq
    n_pp = min(PAGES_PER_STEP, n_pages)
    while n_pages % n_pp:
        n_pp -= 1
    n_rows = 2 * n_heads * t
    assert (2 * t) % 16 == 0 and t <= LANES

    def page_spec(i):
        return pl.BlockSpec((None, n_heads, dh, page),
                            lambda s, c, pt, i=i: (pt[s * n_pages + c * n_pp + i], 0, 0, 0))

    seq_spec = pl.BlockSpec((None, n_heads, t, dh), lambda s, c, pt: (s, 0, 0, 0))
    const = lambda shape: pl.BlockSpec(shape, lambda s, c, pt: (0,) * len(shape))
    grid_spec = pltpu.PrefetchScalarGridSpec(
        num_scalar_prefetch=1,
        grid=(n_seq, n_pages // n_pp),
        in_specs=[const(lam_vecs.shape), const(gsub.shape), seq_spec, seq_spec, seq_spec]
                 + [page_spec(i) for i in range(n_pp)] * 2,
        out_specs=seq_spec,
        scratch_shapes=[pltpu.VMEM((n_rows, dh), F32), pltpu.VMEM((n_rows, 1), F32),
                        pltpu.VMEM((n_rows, 1), F32), pltpu.VMEM((n_rows, dh), F32)],
    )
    return pl.pallas_call(
        functools.partial(_sample_attn_kernel, n_pp=n_pp, t=t, past=n_pages * page, lam_init=lam_init),
        grid_spec=grid_spec,
        out_shape=jax.ShapeDtypeStruct((n_seq, n_heads, t, dh), F32),
        compiler_params=_params(2),
        name="sample_attn",
    )(page_ids, lam_vecs, gsub, q4, k_new4, v_new4, *([cache_kt] * n_pp), *([cache_vt] * n_pp))


def _lru_kernel(lx_ref, lg_ref, cprev_ref, h0_ref, cw_ref, cb_ref, wa_ref, ba_ref, wx_ref, bx_ref, lam_ref,
                out_ref, hlast_ref, ext_scr, h_scr, a_scr, g_scr, *, g, tt):
    i = pl.program_id(1)
    w = LRU_WIDTH
    hist = LRU_CONV - 1
    top = SUBLANES

    @pl.when(i == 0)
    def _():
        ext_scr[:, top - hist:top, :] = cprev_ref[...]
        h_scr[...] = h0_ref[...]

    x3 = lx_ref[...]
    ext_scr[:, top:top + tt, :] = x3
    xc3 = cb_ref[...]
    for j in range(LRU_CONV):
        xc3 = xc3 + ext_scr[:, top - hist + j:top - hist + j + tt, :] * cw_ref[j:j + 1, :]
    ext_scr[:, top - hist:top, :] = ext_scr[:, top + tt - hist:top + tt, :]

    r = g * tt
    xc = xc3.reshape(r, w)
    xb = xc.astype(BF16)
    rg = jax.nn.sigmoid(_dot(xb, wa_ref[...]) + ba_ref[...])
    ig = jax.nn.sigmoid(_dot(xb, wx_ref[...]) + bx_ref[...])
    nl = -lam_ref[...]
    softplus = jnp.maximum(nl, 0.0) + jnp.log1p(jnp.exp(-jnp.abs(nl)))
    log_a = -LRU_C * rg * softplus
    a = jnp.exp(log_a)
    gx = jnp.sqrt(-jnp.tanh(log_a) * (a * a + 1.0)) * (ig * xc)

    nch = r // SUBLANES
    a3 = a.reshape(nch, SUBLANES, w)
    g3 = gx.reshape(nch, SUBLANES, w)
    sub = lax.broadcasted_iota(jnp.int32, (nch, SUBLANES, w), 1)
    for s in (1, 2, 4):
        keep = sub >= s
        a_prev = jnp.where(keep, pltpu.roll(a3, s, 1), 1.0)
        g_prev = jnp.where(keep, pltpu.roll(g3, s, 1), 0.0)
        g3 = a3 * g_prev + g3
        a3 = a3 * a_prev

    nc = tt // SUBLANES
    a_scr[...] = a3.reshape(g, nc, SUBLANES, w)
    g_scr[...] = g3.reshape(g, nc, SUBLANES, w)

    def chunk(c, h):
        hs = a_scr[:, c] * h + g_scr[:, c]
        g_scr[:, c] = hs
        return hs[:, SUBLANES - 1:SUBLANES, :]

    h = h_scr[...]
    if nc <= 2:
        for c in range(nc):
            h = chunk(c, h)
    else:
        h = lax.fori_loop(0, nc, chunk, h)
    h_scr[...] = h

    hs = g_scr[...].reshape(r, w)
    out = hs * jax.nn.gelu(lg_ref[...].reshape(r, w))
    out_ref[...] = out.reshape(g, tt, w).astype(out_ref.dtype)

    @pl.when(i == pl.num_programs(1) - 1)
    def _():
        hlast_ref[...] = h


def _lru(lx3, lg3, conv_prev, h0, cw, cb, wa, ba, wx, bx, lam, g, tt, out_dtype):
    n_seq, t, w = lx3.shape
    blk = pl.BlockSpec((g, tt, w), lambda s, i: (s, i, 0))
    seq = lambda rows: pl.BlockSpec((g, rows, w), lambda s, i: (s, 0, 0))
    return pl.pallas_call(
        functools.partial(_lru_kernel, g=g, tt=tt),
        grid=(n_seq // g, t // tt),
        in_specs=[blk, blk, seq(LRU_CONV - 1), seq(1)] + [_const_spec(a.shape) for a in (cw, cb, wa, ba, wx, bx, lam)],
        out_specs=[blk, seq(1)],
        out_shape=[jax.ShapeDtypeStruct((n_seq, t, w), out_dtype), jax.ShapeDtypeStruct((n_seq, 1, w), F32)],
        scratch_shapes=[pltpu.VMEM((g, SUBLANES + tt, w), F32), pltpu.VMEM((g, 1, w), F32),
                        pltpu.VMEM((g, tt // SUBLANES, SUBLANES, w), F32),
                        pltpu.VMEM((g, tt // SUBLANES, SUBLANES, w), F32)],
        compiler_params=_params(2),
        name="lru",
    )(lx3, lg3, conv_prev, h0, cw, cb, wa, ba, wx, bx, lam)


def _out_cross_kernel(x_ref, da_ref, lru_ref, mk_ref, mv_ref, wo_da_ref, wo_lru_ref, gn_ref, wcq_ref, gcq_ref,
                      wco_ref, out_ref, q_scr, o_scr, *, g, tt):
    r = g * tt
    d = x_ref.shape[-1]
    h = (x_ref[...].reshape(r, d)
         + _dot(da_ref[...].reshape(r, DA_WIDTH).astype(BF16), wo_da_ref[...])
         + _dot(lru_ref[...].reshape(r, LRU_WIDTH).astype(BF16), wo_lru_ref[...]))
    hn = _rmsnorm(h, gn_ref[...]).astype(BF16)
    qc = _dot(hn, wcq_ref[...])
    for hd in range(N_HEADS_X):
        sl = slice(hd * X_HEAD, (hd + 1) * X_HEAD)
        q_scr[:, sl] = _rmsnorm(qc[:, sl], gcq_ref[...])

    if mk_ref.shape[-1] == X_WIDTH:
        for hd in range(N_HEADS_X):
            sl = slice(hd * X_HEAD, (hd + 1) * X_HEAD)
            s = _dot_nt(q_scr[:, sl].astype(BF16), mk_ref[0, :, sl].astype(BF16))
            e = jnp.exp(s - jnp.max(s, axis=-1, keepdims=True))
            o = _dot(e.astype(BF16), mv_ref[0, :, sl].astype(BF16))
            o_scr[:, sl] = o / jnp.sum(e, axis=-1, keepdims=True)
    else:
        n_rows = N_HEADS_X * tt
        n_cols = mk_ref.shape[1]
        col_head = lax.broadcasted_iota(jnp.int32, (n_rows, n_cols), 1) % N_HEADS_X
        own_head = col_head == lax.broadcasted_iota(jnp.int32, (n_rows, n_cols), 0) // tt

        def one_seq(si, carry):
            r0 = pl.multiple_of(si * tt, tt)
            q = q_scr[pl.ds(r0, tt), :]
            qs = jnp.concatenate([q[:, hd * X_HEAD:(hd + 1) * X_HEAD] for hd in range(N_HEADS_X)], axis=0)
            s = jnp.where(own_head, _dot_nt(qs.astype(BF16), mk_ref[si].astype(BF16)), NEG)
            e = jnp.exp(s - jnp.max(s, axis=-1, keepdims=True))
            o = _dot(e.astype(BF16), mv_ref[si].astype(BF16)) / jnp.sum(e, axis=-1, keepdims=True)
            for hd in range(N_HEADS_X):
                o_scr[pl.ds(r0, tt), hd * X_HEAD:(hd + 1) * X_HEAD] = o[hd * tt:(hd + 1) * tt, :]
            return carry

        lax.fori_loop(0, g, one_seq, 0)

    out = h + _dot(o_scr[...].astype(BF16), wco_ref[...])
    out_ref[...] = out.reshape(g, tt, d)


def _out_cross(x3, da3, lru3, mk3, mv3, mem_first, wo_da, wo_lru, gn, wcq, gcq, wco, g, tt):
    n_seq, t, d = x3.shape
    assert mk3.shape[-1] == X_HEAD or (mk3.shape[-1] == X_WIDTH and g == 1)
    mem_blk0 = mem_first // g
    blk = lambda w: pl.BlockSpec((g, tt, w), lambda s, i: (s, i, 0))
    mem = pl.BlockSpec((g,) + mk3.shape[1:], lambda s, i: (s + mem_blk0, 0, 0))
    return pl.pallas_call(
        functools.partial(_out_cross_kernel, g=g, tt=tt),
        grid=(n_seq // g, t // tt),
        in_specs=[blk(d), blk(DA_WIDTH), blk(LRU_WIDTH), mem, mem]
                 + [_const_spec(a.shape) for a in (wo_da, wo_lru, gn, wcq, gcq, wco)],
        out_specs=blk(d),
        out_shape=jax.ShapeDtypeStruct((n_seq, t, d), F32),
        scratch_shapes=[pltpu.VMEM((g * tt, X_WIDTH), F32), pltpu.VMEM((g * tt, X_WIDTH), F32)],
        compiler_params=_params(2),
        name="out_cross",
    )(x3, da3, lru3, mk3, mv3, wo_da, wo_lru, gn, wcq, gcq, wco)


def _conv_ffn_kernel(x_ref, prev_ref, gn_ref, wup_ref, cw_ref, cb_ref, wdn_ref, out_ref, state_ref,
                     halo_scr, ext_scr, *, g, tt, fc):
    i = pl.program_id(1)
    r = g * tt
    d = x_ref.shape[-1]
    f = wdn_ref.shape[0]
    hist = FFN_CONV - 1
    top = SUBLANES

    @pl.when(i == 0)
    def _():
        halo_scr[:, top - hist:top, :] = prev_ref[...]

    x = x_ref[...].reshape(r, d)
    xn = _rmsnorm(x, gn_ref[...]).astype(BF16)
    out_ref[...] = x_ref[...]

    for ch in range(f // fc):
        halves = []
        for half in range(2):
            c0 = half * f + ch * fc
            cols = slice(c0, c0 + fc)
            u3 = _dot(xn, wup_ref[:, cols]).reshape(g, tt, fc)
            ext_scr[:, top:top + tt, :] = u3
            ext_scr[:, top - hist:top, :] = halo_scr[:, top - hist:top, cols]
            c3 = cb_ref[:, cols]
            for j in range(FFN_CONV):
                c3 = c3 + ext_scr[:, top - hist + j:top - hist + j + tt, :] * cw_ref[j:j + 1, cols]
            halo_scr[:, top - hist:top, cols] = ext_scr[:, top + tt - hist:top + tt, :]
            halves.append(c3.reshape(r, fc))
        act = (jax.nn.silu(halves[0]) * halves[1]).astype(BF16)
        out_ref[...] += _dot(act, wdn_ref[ch * fc:(ch + 1) * fc, :]).reshape(g, tt, d)

    @pl.when(i == pl.num_programs(1) - 1)
    def _():
        state_ref[...] = halo_scr[:, top - hist:top, :]


def _conv_ffn(x3, prev, gn, wup, cw, cb, wdn, g, tt):
    n_seq, t, d = x3.shape
    f2 = wup.shape[1]
    fc = _tile(f2 // 2, FFN_CHUNK)
    blk = pl.BlockSpec((g, tt, d), lambda s, i: (s, i, 0))
    st = pl.BlockSpec((g, FFN_CONV - 1, f2), lambda s, i: (s, 0, 0))
    return pl.pallas_call(
        functools.partial(_conv_ffn_kernel, g=g, tt=tt, fc=fc),
        grid=(n_seq // g, t // tt),
        in_specs=[blk, st] + [_const_spec(a.shape) for a in (gn, wup, cw, cb, wdn)],
        out_specs=[blk, st],
        out_shape=[jax.ShapeDtypeStruct((n_seq, t, d), F32), jax.ShapeDtypeStruct((n_seq, FFN_CONV - 1, f2), F32)],
        scratch_shapes=[pltpu.VMEM((g, SUBLANES, f2), F32), pltpu.VMEM((g, SUBLANES + tt, fc), F32)],
        compiler_params=_params(2),
        name="conv_ffn",
    )(x3, prev, gn, wup, cw, cb, wdn)


def _block_diag(blocks):
    n, a, b = blocks.shape
    eye = jnp.eye(n, dtype=blocks.dtype)
    return (eye[:, None, :, None] * blocks[:, :, None, :]).reshape(n * a, n * b)


def _layer_weights(l, norm_mix, w_in, g_q, g_k, lam_q1, lam_k1, lam_q2, lam_k2, g_sub, lru_conv_w, lru_conv_b,
                   lru_wa, lru_ba, lru_wx, lru_bx, lru_lambda, w_o, norm_cross, norm_mem, w_cq, w_ck, w_cv,
                   g_cq, g_ck, w_co, norm_ffn, w_up, ffn_conv_w, ffn_conv_b, w_down):
    row = lambda a: a[l].reshape(1, -1)
    n_maps = DA_WIDTH // DA_HALF
    return dict(
        norm_mix=row(norm_mix), w_in=w_in[l].astype(BF16),
        pmat=jnp.kron(jnp.eye(n_maps, dtype=F32), jnp.full((DA_HALF, DA_HALF), 1.0 / DA_HALF, F32)).astype(BF16),
        gq=jnp.tile(g_q[l] * (DA_HALF ** -0.5), n_maps).reshape(1, -1),
        gq_base2=jnp.tile(g_q[l] * (DA_HALF ** -0.5 * LOG2E), n_maps).reshape(1, -1),
        gk=jnp.tile(g_k[l], n_maps).reshape(1, -1),
        lam=jnp.stack([lam_q1[l], lam_k1[l], lam_q2[l], lam_k2[l]]),
        gsub2=jnp.tile(g_sub[l], LANES // DA_HEAD).reshape(1, -1),
        gsub=row(g_sub),
        cw=lru_conv_w[l], cb=row(lru_conv_b),
        wa=_block_diag(lru_wa[l]).astype(BF16), ba=row(lru_ba),
        wx=_block_diag(lru_wx[l]).astype(BF16), bx=row(lru_bx), lru_lambda=row(lru_lambda),
        wo_da=w_o[l, :DA_WIDTH].astype(BF16), wo_lru=w_o[l, DA_WIDTH:].astype(BF16),
        norm_cross=row(norm_cross), norm_mem=row(norm_mem),
        wcq=w_cq[l].astype(BF16), wck=w_ck[l].astype(BF16), wcv=w_cv[l].astype(BF16),
        gcq=(g_cq[l] * (X_HEAD ** -0.5)).reshape(1, -1), gck=row(g_ck), wco=w_co[l].astype(BF16),
        norm_ffn=row(norm_ffn), wup=w_up[l].astype(BF16), fcw=ffn_conv_w[l], fcb=row(ffn_conv_b),
        wdn=w_down[l].astype(BF16),
    )


def _mix_and_ffn(x3, da3, lx, lg, conv_prev, h0, ffn_prev, mk3, mv3, mem_first, p, g, tt, mid_dtype):
    n_seq, t, d = x3.shape
    assert t >= LRU_CONV - 1 and t >= FFN_CONV - 1
    lru3, h_last = _lru(lx.reshape(n_seq, t, LRU_WIDTH), lg.reshape(n_seq, t, LRU_WIDTH), conv_prev, h0,
                        p['cw'], p['cb'], p['wa'], p['ba'], p['wx'], p['bx'], p['lru_lambda'], g, tt, mid_dtype)
    g_x = g
    while g_x > 1 and 4 * g_x * mk3.shape[1] * mk3.shape[2] * 4 > MEM_BLOCK_BYTES:
        g_x //= 2
    h3 = _out_cross(x3, da3, lru3, mk3, mv3, mem_first, p['wo_da'], p['wo_lru'], p['norm_cross'], p['wcq'], p['gcq'],
                    p['wco'], g_x, tt)
    y3, ffn_state = _conv_ffn(h3, ffn_prev, p['norm_ffn'], p['wup'], p['fcw'], p['fcb'], p['wdn'], g, tt)
    conv_state = lx.reshape(n_seq, t, LRU_WIDTH)[:, t - (LRU_CONV - 1):]
    return y3, conv_state, h_last.reshape(n_seq, LRU_WIDTH), ffn_state


def kernel(x_prompt, x_sample, mem_prompt, cache_k, cache_v, page_table, cache_mem_k, cache_mem_v, state_lru_conv, state_lru_h, state_ffn_conv, norm_mix, w_in, g_q, g_k, lam_q1, lam_k1, lam_q2, lam_k2, g_sub, lru_conv_w, lru_conv_b, lru_wa, lru_ba, lru_wx, lru_bx, lru_lambda, w_o, norm_cross, norm_mem, w_cq, w_ck, w_cv, g_cq, g_ck, w_co, norm_ffn, w_up, ffn_conv_w, ffn_conv_b, w_down):
    depth = w_in.shape[0]
    b, t, d = x_prompt.shape
    nb, nt, _ = x_sample.shape
    n_mem = mem_prompt.shape[1]
    n_phys, page = cache_k.shape[1], cache_k.shape[2]
    f2 = w_up.shape[2]
    to_pages = lambda c: jnp.transpose(c, (0, 1, 3, 4, 2)).reshape(depth * n_phys, N_HEADS_DA, DA_HEAD, page)
    cache_kt, cache_vt = to_pages(cache_k), to_pages(cache_v)
    to_heads = lambda a, n, m: jnp.transpose(a.reshape(n, m, N_HEADS_DA, DA_HEAD), (0, 2, 1, 3))
    cmk = cache_mem_k.reshape(depth * nb, n_mem * N_HEADS_X, X_HEAD)
    cmv = cache_mem_v.reshape(depth * nb, n_mem * N_HEADS_X, X_HEAD)

    tt_p = _tile(t, ROW_TILE)
    g_s = SAMPLE_SEQS if nb % SAMPLE_SEQS == 0 else 1

    yp, ys = x_prompt, x_sample
    outs_p, outs_s = [], []
    for l in range(depth):
        p = _layer_weights(l, norm_mix, w_in, g_q, g_k, lam_q1, lam_k1, lam_q2, lam_k2, g_sub, lru_conv_w,
                           lru_conv_b, lru_wa, lru_ba, lru_wx, lru_bx, lru_lambda, w_o, norm_cross, norm_mem,
                           w_cq, w_ck, w_cv, g_cq, g_ck, w_co, norm_ffn, w_up, ffn_conv_w, ffn_conv_b, w_down)
        lam_init = 0.8 - 0.6 * math.exp(-0.3 * l)

        q, k, v, lx, lg = _in_proj(yp.reshape(b * t, d), p['norm_mix'], p['w_in'], p['pmat'], p['gq_base2'], p['gk'],
                                   BF16)
        da = _prompt_attn(q, k, v, p['lam'], p['gsub2'], b, t, lam_init)
        mk, mv = _mem_kv(mem_prompt.reshape(b * n_mem, d), p['norm_mem'], p['wck'], p['wcv'], p['gck'])
        yp, conv_state, h_last, ffn_state = _mix_and_ffn(
            yp, da.reshape(b, t, DA_WIDTH), lx, lg,
            jnp.zeros((b, LRU_CONV - 1, LRU_WIDTH), F32), jnp.zeros((b, 1, LRU_WIDTH), F32),
            jnp.zeros((b, FFN_CONV - 1, f2), F32),
            mk.reshape(b, n_mem, X_WIDTH), mv.reshape(b, n_mem, X_WIDTH), 0, p, 1, tt_p, BF16)
        outs_p.append((k.reshape(b, t, N_HEADS_DA, DA_HEAD), v.reshape(b, t, N_HEADS_DA, DA_HEAD),
                       mk.reshape(b, n_mem, N_HEADS_X, X_HEAD), mv.reshape(b, n_mem, N_HEADS_X, X_HEAD),
                       conv_state, h_last, ffn_state))

        q, k, v, lx, lg = _in_proj(ys.reshape(nb * nt, d), p['norm_mix'], p['w_in'], p['pmat'], p['gq'], p['gk'], F32)
        page_ids = page_table.reshape(-1).astype(jnp.int32) + l * n_phys
        da = _sample_attn(page_ids, to_heads(q, nb, nt), to_heads(k, nb, nt), to_heads(v, nb, nt),
                          cache_kt, cache_vt, p['lam'], p['gsub'], lam_init)
        da = jnp.transpose(da, (0, 2, 1, 3))
        ys, conv_state, h_last, ffn_state = _mix_and_ffn(
            ys, da.reshape(nb, nt, DA_WIDTH), lx, lg,
            state_lru_conv[l], state_lru_h[l].reshape(nb, 1, LRU_WIDTH), state_ffn_conv[l],
            cmk, cmv, l * nb, p, g_s, nt, F32)
        outs_s.append((k.reshape(nb, nt, N_HEADS_DA, DA_HEAD), v.reshape(nb, nt, N_HEADS_DA, DA_HEAD),
                       conv_state, h_last, ffn_state))

    stack = lambda outs, j: jnp.stack([o[j] for o in outs])
    return (yp, ys, stack(outs_p, 0), stack(outs_p, 1), stack(outs_s, 0), stack(outs_s, 1),
            stack(outs_p, 2), stack(outs_p, 3), stack(outs_p, 4), stack(outs_s, 2),
            stack(outs_p, 5), stack(outs_s, 3), stack(outs_p, 6), stack(outs_s, 4))
```

```python
import functools
import math

import jax
import jax.numpy as jnp
from jax import lax
from jax.experimental import pallas as pl
from jax.experimental.pallas import tpu as pltpu

F32 = jnp.float32
BF16 = jnp.bfloat16

EPS = 1e-6
N_HEADS_DA = 8
DA_HEAD = 64
DA_HALF = DA_HEAD // 2
DA_WIDTH = N_HEADS_DA * DA_HEAD
LRU_WIDTH = 512
LRU_CONV = 4
LRU_C = 8.0
N_HEADS_X = 4
X_HEAD = 128
X_WIDTH = N_HEADS_X * X_HEAD
FFN_CONV = 3

LANES = 128
SUBLANES = 8
NEG = -1e30
VMEM_LIMIT = 48 * 1024 * 1024

ROW_TILE = 512
Q_TILE = 256
PAGES_PER_STEP = 16
FFN_CHUNK = 1408
SAMPLE_SEQS = 32
MEM_BLOCK_BYTES = 16 * 1024 * 1024


def _tile(n, pref):
    t = min(n, pref)
    while n % t:
        t -= SUBLANES
    return t


def _const_spec(shape):
    zeros = (0,) * len(shape)
    return pl.BlockSpec(shape, lambda *_: zeros, pipeline_mode=pl.Buffered(1))


def _params(n_axes):
    return pltpu.CompilerParams(dimension_semantics=("arbitrary",) * n_axes, vmem_limit_bytes=VMEM_LIMIT)


def _rmsnorm(x, g):
    return x * lax.rsqrt(jnp.mean(x * x, axis=-1, keepdims=True) + EPS) * g


def _sigmoid(x):
    return 0.5 * jnp.tanh(0.5 * x) + 0.5


def _dot(a, b):
    return jnp.dot(a, b, preferred_element_type=F32)


def _dot_nt(a, b):
    return lax.dot_general(a, b, (((1,), (1,)), ((), ())), preferred_element_type=F32)


def _diff_lambda(lam_ref, lam_init):
    v = lam_ref[...]
    s1 = jnp.sum(v[0:1] * v[1:2], axis=-1, keepdims=True)
    s2 = jnp.sum(v[2:3] * v[3:4], axis=-1, keepdims=True)
    return jnp.exp(s1) - jnp.exp(s2) + lam_init


def _in_proj_kernel(x_ref, gn_ref, w_ref, p_ref, gq_ref, gk_ref, q_ref, k_ref, v_ref, lx_ref, lg_ref, *scratch):
    xn = _rmsnorm(x_ref[...], gn_ref[...]).astype(BF16)

    def proj(j):
        return _dot(xn, w_ref[:, j * DA_WIDTH:(j + 1) * DA_WIDTH])

    def map_norm(z, g):
        sq = z * z
        hi = sq.astype(BF16)
        lo = (sq - hi.astype(F32)).astype(BF16)
        ms = _dot(hi, p_ref[...]) + _dot(lo, p_ref[...])
        return z * lax.rsqrt(ms + EPS) * g

    q_ref[...] = map_norm(proj(0), gq_ref[...]).astype(q_ref.dtype)
    k = map_norm(proj(1), gk_ref[...])
    if scratch:
        v_scr, = scratch
        v_scr[...] = proj(2)
        k_ref[...] = k.T
        v_ref[...] = v_scr[...].T
    else:
        k_ref[...] = k
        v_ref[...] = proj(2)
    lx_ref[...] = proj(3)
    lg_ref[...] = proj(4)


def _in_proj(x2d, gn, w_in, pmat, gq, gk, q_dtype, seq_len=None):
    n, d = x2d.shape
    r = _tile(seq_len or n, ROW_TILE)
    row = lambda w: pl.BlockSpec((r, w), lambda i: (i, 0))
    out_sd = lambda dt: jax.ShapeDtypeStruct((n, DA_WIDTH), dt)
    kv_spec, kv_sd = row(DA_WIDTH), out_sd(F32)
    if seq_len:
        tiles = seq_len // r
        kv_spec = pl.BlockSpec((None, DA_WIDTH, r), lambda i: (i // tiles, 0, i % tiles))
        kv_sd = jax.ShapeDtypeStruct((n // seq_len, DA_WIDTH, seq_len), F32)
    return pl.pallas_call(
        _in_proj_kernel,
        grid=(n // r,),
        in_specs=[row(d), _const_spec(gn.shape), _const_spec(w_in.shape), _const_spec(pmat.shape),
                  _const_spec(gq.shape), _const_spec(gk.shape)],
        out_specs=[row(DA_WIDTH), kv_spec, kv_spec, row(DA_WIDTH), row(DA_WIDTH)],
        out_shape=[out_sd(q_dtype), kv_sd, kv_sd, out_sd(F32), out_sd(F32)],
        scratch_shapes=[pltpu.VMEM((r, DA_WIDTH), F32)] if seq_len else [],
        compiler_params=_params(1),
        name="in_proj",
    )(x2d, gn, w_in, pmat, gq, gk)


def _mem_kv_kernel(m_ref, gn_ref, wk_ref, wv_ref, gck_ref, k_ref, v_ref):
    mn = _rmsnorm(m_ref[...], gn_ref[...]).astype(BF16)
    k = _dot(mn, wk_ref[...])
    for h in range(N_HEADS_X):
        sl = slice(h * X_HEAD, (h + 1) * X_HEAD)
        k_ref[:, sl] = _rmsnorm(k[:, sl], gck_ref[...])
    v_ref[...] = _dot(mn, wv_ref[...])


def _mem_kv(mem2d, gn, w_ck, w_cv, g_ck):
    n, d = mem2d.shape
    r = _tile(n, ROW_TILE)
    row = lambda w: pl.BlockSpec((r, w), lambda i: (i, 0))
    return pl.pallas_call(
        _mem_kv_kernel,
        grid=(n // r,),
        in_specs=[row(d), _const_spec(gn.shape), _const_spec(w_ck.shape), _const_spec(w_cv.shape),
                  _const_spec(g_ck.shape)],
        out_specs=[row(X_WIDTH)] * 2,
        out_shape=[jax.ShapeDtypeStruct((n, X_WIDTH), F32)] * 2,
        compiler_params=_params(1),
        name="mem_kv",
    )(mem2d, gn, w_ck, w_cv, g_ck)


def _head_slope(head, shape):
    e = (head + 1).astype(F32) * (-8.0 / N_HEADS_DA)
    return jnp.exp2(jnp.full(shape, e, F32))


def _sub_norm_pair(o, lane, gsub, lam_init):
    sq = o * o
    first = lane < DA_HEAD
    ms0 = jnp.sum(jnp.where(first, sq, 0.0), axis=-1, keepdims=True)
    ms1 = jnp.sum(jnp.where(first, 0.0, sq), axis=-1, keepdims=True)
    ms = jnp.where(first, ms0, ms1) * (1.0 / DA_HEAD)
    return (o * lax.rsqrt(ms + EPS) * gsub) * (1.0 - lam_init)


BIAS_SPLIT = 256


N_BIAS_TERMS = 3
LOG2E = math.log2(math.e)


def _prompt_attn_kernel(lam_ref, gsub_ref, q_ref, kt_ref, vt_ref, o_ref, kb_scr, vb_scr, *, tq, lam_init):
    hp = pl.program_id(1)
    t = kt_ref.shape[1]
    other = lambda hh: (1 - hh) * DA_HEAD

    slab_row = lax.broadcasted_iota(jnp.int32, (LANES, t), 0)
    pos = lax.broadcasted_iota(jnp.int32, (LANES, t), 1)
    pos_hi = (pos // BIAS_SPLIT * BIAS_SPLIT).astype(F32)
    pos_lo = (pos % BIAS_SPLIT).astype(F32)
    for hh in range(2):
        own = (slab_row >= hh * DA_HEAD) & (slab_row < (hh + 1) * DA_HEAD)
        rel = slab_row - other(hh)
        in_bias = (rel >= 0) & (rel < 2 * N_BIAS_TERMS)
        bias = jnp.where(in_bias, jnp.where(rel % 2 == 0, pos_hi, pos_lo), 0.0)
        kb_scr[hh] = jnp.where(own, kt_ref[...], bias).astype(BF16)
        vb_scr[hh] = jnp.where(own, vt_ref[...], 1.0).astype(BF16)

    lane = lax.broadcasted_iota(jnp.int32, (tq, LANES), 1)
    visible = (lax.broadcasted_iota(jnp.int32, (tq, tq), 1) <= lax.broadcasted_iota(jnp.int32, (tq, tq), 0))
    lam = _diff_lambda(lam_ref, lam_init)

    def q_slab(q, hh, c):
        lo = hh * DA_HEAD + c * DA_HALF
        rel = lane - other(hh)
        rest = _head_slope(hp * 2 + hh, (1, LANES)) * LOG2E
        factor = jnp.zeros((tq, LANES), F32)
        for piece in range(N_BIAS_TERMS):
            part = rest.astype(BF16).astype(F32)
            factor = jnp.where((rel >= 2 * piece) & (rel < 2 * piece + 2), part, factor)
            rest = rest - part
        return jnp.where((lane >= lo) & (lane < lo + DA_HALF), q, factor).astype(BF16)

    for qi in range(t // tq):
        rows = slice(qi * tq, (qi + 1) * tq)
        past = qi * tq
        q = q_ref[rows, :].astype(F32)
        heads = []
        for hh in range(2):
            own = (lane >= hh * DA_HEAD) & (lane < (hh + 1) * DA_HEAD)
            maps = []
            for c in range(2):
                qm = q_slab(q, hh, c)
                s_diag = jnp.where(visible, _dot(qm, kb_scr[hh, :, rows]), NEG)
                m = jnp.max(s_diag, axis=-1, keepdims=True)
                if past:
                    s_past = _dot(qm, kb_scr[hh, :, :past])
                    m = jnp.maximum(m, jnp.max(s_past, axis=-1, keepdims=True))
                acc = _dot_nt(jnp.exp2(s_diag - m).astype(BF16), vb_scr[hh, :, rows])
                if past:
                    acc = acc + _dot_nt(jnp.exp2(s_past - m).astype(BF16), vb_scr[hh, :, :past])
                maps.append(acc / jnp.where(own, pltpu.roll(acc, DA_HEAD, 1), 1.0))
            heads.append(maps[0] - lam * maps[1])
        o = jnp.where(lane < DA_HEAD, heads[0], heads[1])
        o_ref[rows, :] = _sub_norm_pair(o, lane, gsub_ref[...], lam_init).astype(o_ref.dtype)


def _prompt_attn(q, kt, vt, lam_vecs, gsub2, lam_init):
    b, _, t = kt.shape
    tq = _tile(t, Q_TILE)
    n_pairs = DA_WIDTH // LANES
    assert t <= BIAS_SPLIT * 256
    spec = pl.BlockSpec((t, LANES), lambda bi, hp: (bi, hp))
    spec_t = pl.BlockSpec((None, LANES, t), lambda bi, hp: (bi, hp, 0))
    return pl.pallas_call(
        functools.partial(_prompt_attn_kernel, tq=tq, lam_init=lam_init),
        grid=(b, n_pairs),
        in_specs=[_const_spec(lam_vecs.shape), _const_spec(gsub2.shape), spec, spec_t, spec_t],
        out_specs=spec,
        out_shape=jax.ShapeDtypeStruct((b * t, DA_WIDTH), BF16),
        scratch_shapes=[pltpu.VMEM((2, LANES, t), BF16), pltpu.VMEM((2, LANES, t), BF16)],
        compiler_params=_params(2),
        name="prompt_attn",
    )(lam_vecs, gsub2, q, kt, vt)


def _sample_attn_kernel(pt_ref, lam_ref, gsub_ref, q_ref, kn_ref, vn_ref, *refs, n_pp, t, past, lam_init):
    k_pages = refs[:n_pp]
    v_pages = refs[n_pp:2 * n_pp]
    o_ref = refs[2 * n_pp]
    q_scr, m_scr, l_scr, acc_scr = refs[2 * n_pp + 1:]
    del pt_ref
    ci = pl.program_id(1)
    rows_h = 2 * t
    n_rows = N_HEADS_DA * rows_h
    tk = n_pp * k_pages[0].shape[-1]
    head_rows = lambda h: slice(h * rows_h, (h + 1) * rows_h)

    @pl.when(ci == 0)
    def _():
        first_map = lax.broadcasted_iota(jnp.int32, (t, DA_HEAD), 1) < DA_HALF
        for h in range(N_HEADS_DA):
            qh = q_ref[h]
            q_scr[h * rows_h:h * rows_h + t, :] = jnp.where(first_map, qh, 0.0)
            q_scr[h * rows_h + t:(h + 1) * rows_h, :] = jnp.where(first_map, 0.0, qh)
        m_scr[...] = jnp.full(m_scr.shape, NEG, F32)
        l_scr[...] = jnp.zeros(l_scr.shape, F32)
        acc_scr[...] = jnp.zeros(acc_scr.shape, F32)

    r_idx = lax.broadcasted_iota(jnp.int32, (n_rows, 1), 0)
    slope = jnp.exp2((r_idx // rows_h + 1).astype(F32) * (-8.0 / N_HEADS_DA))
    q_pos = r_idx % t

    def scores(keys_t):
        return jnp.concatenate(
            [_dot(q_scr[head_rows(h), :].astype(BF16), keys_t(h)) for h in range(N_HEADS_DA)], axis=0)

    def update(s, weighted_values):
        m_old = m_scr[...]
        m_new = jnp.maximum(m_old, jnp.max(s, axis=-1, keepdims=True))
        alpha = jnp.exp(m_old - m_new)
        p = jnp.exp(s - m_new)
        l_scr[...] = alpha * l_scr[...] + jnp.sum(p, axis=-1, keepdims=True)
        pb = p.astype(BF16)
        pv = jnp.concatenate([weighted_values(h, pb[head_rows(h), :]) for h in range(N_HEADS_DA)], axis=0)
        acc_scr[...] = alpha * acc_scr[...] + pv
        m_scr[...] = m_new

    key_off = lax.broadcasted_iota(jnp.int32, (1, tk), 1) + (ci * tk - past)
    s = scores(lambda h: jnp.concatenate([kp[h] for kp in k_pages], axis=1).astype(BF16))
    s = s - slope * (q_pos - key_off).astype(F32)
    update(s, lambda h, ph: _dot_nt(ph, jnp.concatenate([vp[h] for vp in v_pages], axis=1).astype(BF16)))

    @pl.when(ci == pl.num_programs(1) - 1)
    def _():
        pad = jnp.zeros((LANES - t, DA_HEAD), F32)
        padded = lambda ref, h: jnp.concatenate([ref[h], pad], axis=0).astype(BF16)
        new_i = lax.broadcasted_iota(jnp.int32, (1, LANES), 1)
        s = jnp.concatenate(
            [_dot_nt(q_scr[head_rows(h), :].astype(BF16), padded(kn_ref, h)) for h in range(N_HEADS_DA)], axis=0)
        s = s - slope * (q_pos - new_i).astype(F32)
        s = jnp.where((new_i < t) & (new_i <= q_pos), s, NEG)
        update(s, lambda h, ph: _dot(ph, padded(vn_ref, h)))

        o = acc_scr[...] / l_scr[...]
        lam = _diff_lambda(lam_ref, lam_init)
        for h in range(N_HEADS_DA):
            d = o[h * rows_h:h * rows_h + t, :] - lam * o[h * rows_h + t:(h + 1) * rows_h, :]
            o_ref[h] = _rmsnorm(d, gsub_ref[...]) * (1.0 - lam_init)


def _sample_attn(page_ids, q4, k_new4, v_new4, cache_kt, cache_vt, lam_vecs, gsub, lam_init):
    n_seq, n_heads, t, dh = q4.shape
    page = cache_kt.shape[-1]
    n_pages = page_ids.shape[0] // n_seq
    n_pp = min(PAGES_PER_STEP, n_pages)
    while n_pages % n_pp:
        n_pp -= 1
    n_rows = 2 * n_heads * t
    assert (2 * t) % 16 == 0 and t <= LANES

    def page_spec(i):
        return pl.BlockSpec((None, n_heads, dh, page),
                            lambda s, c, pt, i=i: (pt[s * n_pages + c * n_pp + i], 0, 0, 0))

    seq_spec = pl.BlockSpec((None, n_heads, t, dh), lambda s, c, pt: (s, 0, 0, 0))
    const = lambda shape: pl.BlockSpec(shape, lambda s, c, pt: (0,) * len(shape))
    grid_spec = pltpu.PrefetchScalarGridSpec(
        num_scalar_prefetch=1,
        grid=(n_seq, n_pages // n_pp),
        in_specs=[const(lam_vecs.shape), const(gsub.shape), seq_spec, seq_spec, seq_spec]
                 + [page_spec(i) for i in range(n_pp)] * 2,
        out_specs=seq_spec,
        scratch_shapes=[pltpu.VMEM((n_rows, dh), F32), pltpu.VMEM((n_rows, 1), F32),
                        pltpu.VMEM((n_rows, 1), F32), pltpu.VMEM((n_rows, dh), F32)],
    )
    return pl.pallas_call(
        functools.partial(_sample_attn_kernel, n_pp=n_pp, t=t, past=n_pages * page, lam_init=lam_init),
        grid_spec=grid_spec,
        out_shape=jax.ShapeDtypeStruct((n_seq, n_heads, t, dh), F32),
        compiler_params=_params(2),
        name="sample_attn",
    )(page_ids, lam_vecs, gsub, q4, k_new4, v_new4, *([cache_kt] * n_pp), *([cache_vt] * n_pp))


def _lru_kernel(lx_ref, lg_ref, cprev_ref, h0_ref, cw_ref, cb_ref, wa_ref, ba_ref, wx_ref, bx_ref, lam_ref,
                out_ref, hlast_ref, ext_scr, h_scr, a_scr, g_scr, *, g, tt):
    i = pl.program_id(1)
    w = LRU_WIDTH
    hist = LRU_CONV - 1
    top = SUBLANES

    @pl.when(i == 0)
    def _():
        ext_scr[:, top - hist:top, :] = cprev_ref[...]
        h_scr[...] = h0_ref[...]

    x3 = lx_ref[...]
    ext_scr[:, top:top + tt, :] = x3
    xc3 = cb_ref[...]
    for j in range(LRU_CONV):
        xc3 = xc3 + ext_scr[:, top - hist + j:top - hist + j + tt, :] * cw_ref[j:j + 1, :]
    ext_scr[:, top - hist:top, :] = ext_scr[:, top + tt - hist:top + tt, :]

    r = g * tt
    xc = xc3.reshape(r, w)
    xb = xc.astype(BF16)
    rg = _sigmoid(_dot(xb, wa_ref[...]) + ba_ref[...])
    ig = _sigmoid(_dot(xb, wx_ref[...]) + bx_ref[...])
    nl = -lam_ref[...]
    softplus = jnp.maximum(nl, 0.0) + jnp.log1p(jnp.exp(-jnp.abs(nl)))
    log_a = -LRU_C * rg * softplus
    a = jnp.exp(log_a)
    gx = jnp.sqrt(-jnp.tanh(log_a) * (a * a + 1.0)) * (ig * xc)

    nch = r // SUBLANES
    a3 = a.reshape(nch, SUBLANES, w)
    g3 = gx.reshape(nch, SUBLANES, w)
    sub = lax.broadcasted_iota(jnp.int32, (nch, SUBLANES, w), 1)
    for s in (1, 2, 4):
        keep = sub >= s
        a_prev = jnp.where(keep, pltpu.roll(a3, s, 1), 1.0)
        g_prev = jnp.where(keep, pltpu.roll(g3, s, 1), 0.0)
        g3 = a3 * g_prev + g3
        a3 = a3 * a_prev

    nc = tt // SUBLANES
    a_scr[...] = a3.reshape(g, nc, SUBLANES, w)
    g_scr[...] = g3.reshape(g, nc, SUBLANES, w)

    def chunk(c, h):
        hs = a_scr[:, c] * h + g_scr[:, c]
        g_scr[:, c] = hs
        return hs[:, SUBLANES - 1:SUBLANES, :]

    h = h_scr[...]
    if nc <= 2:
        for c in range(nc):
            h = chunk(c, h)
    else:
        h = lax.fori_loop(0, nc, chunk, h)
    h_scr[...] = h

    hs = g_scr[...].reshape(r, w)
    out = hs * jax.nn.gelu(lg_ref[...].reshape(r, w))
    out_ref[...] = out.reshape(g, tt, w).astype(out_ref.dtype)

    @pl.when(i == pl.num_programs(1) - 1)
    def _():
        hlast_ref[...] = h


def _lru(lx3, lg3, conv_prev, h0, cw, cb, wa, ba, wx, bx, lam, g, tt, out_dtype):
    n_seq, t, w = lx3.shape
    blk = pl.BlockSpec((g, tt, w), lambda s, i: (s, i, 0))
    seq = lambda rows: pl.BlockSpec((g, rows, w), lambda s, i: (s, 0, 0))
    return pl.pallas_call(
        functools.partial(_lru_kernel, g=g, tt=tt),
        grid=(n_seq // g, t // tt),
        in_specs=[blk, blk, seq(LRU_CONV - 1), seq(1)] + [_const_spec(a.shape) for a in (cw, cb, wa, ba, wx, bx, lam)],
        out_specs=[blk, seq(1)],
        out_shape=[jax.ShapeDtypeStruct((n_seq, t, w), out_dtype), jax.ShapeDtypeStruct((n_seq, 1, w), F32)],
        scratch_shapes=[pltpu.VMEM((g, SUBLANES + tt, w), F32), pltpu.VMEM((g, 1, w), F32),
                        pltpu.VMEM((g, tt // SUBLANES, SUBLANES, w), F32),
                        pltpu.VMEM((g, tt // SUBLANES, SUBLANES, w), F32)],
        compiler_params=_params(2),
        name="lru",
    )(lx3, lg3, conv_prev, h0, cw, cb, wa, ba, wx, bx, lam)


def _out_cross_kernel(x_ref, da_ref, lru_ref, mk_ref, mv_ref, wo_da_ref, wo_lru_ref, gn_ref, wcq_ref, gcq_ref,
                      wco_ref, out_ref, q_scr, o_scr, *, g, tt):
    r = g * tt
    d = x_ref.shape[-1]
    h = (x_ref[...].reshape(r, d)
         + _dot(da_ref[...].reshape(r, DA_WIDTH).astype(BF16), wo_da_ref[...])
         + _dot(lru_ref[...].reshape(r, LRU_WIDTH).astype(BF16), wo_lru_ref[...]))
    hn = _rmsnorm(h, gn_ref[...]).astype(BF16)
    qc = _dot(hn, wcq_ref[...])
    for hd in range(N_HEADS_X):
        sl = slice(hd * X_HEAD, (hd + 1) * X_HEAD)
        q_scr[:, sl] = _rmsnorm(qc[:, sl], gcq_ref[...])

    if mk_ref.shape[-1] == X_WIDTH:
        for hd in range(N_HEADS_X):
            sl = slice(hd * X_HEAD, (hd + 1) * X_HEAD)
            s = _dot_nt(q_scr[:, sl].astype(BF16), mk_ref[0, :, sl].astype(BF16))
            e = jnp.exp(s - jnp.max(s, axis=-1, keepdims=True))
            o = _dot(e.astype(BF16), mv_ref[0, :, sl].astype(BF16))
            o_scr[:, sl] = o / jnp.sum(e, axis=-1, keepdims=True)
    else:
        n_rows = N_HEADS_X * tt
        n_cols = mk_ref.shape[1]
        col_head = lax.broadcasted_iota(jnp.int32, (n_rows, n_cols), 1) % N_HEADS_X
        own_head = col_head == lax.broadcasted_iota(jnp.int32, (n_rows, n_cols), 0) // tt

        def one_seq(si, carry):
            r0 = pl.multiple_of(si * tt, tt)
            q = q_scr[pl.ds(r0, tt), :]
            qs = jnp.concatenate([q[:, hd * X_HEAD:(hd + 1) * X_HEAD] for hd in range(N_HEADS_X)], axis=0)
            s = jnp.where(own_head, _dot_nt(qs.astype(BF16), mk_ref[si].astype(BF16)), NEG)
            e = jnp.exp(s - jnp.max(s, axis=-1, keepdims=True))
            o = _dot(e.astype(BF16), mv_ref[si].astype(BF16)) / jnp.sum(e, axis=-1, keepdims=True)
            for hd in range(N_HEADS_X):
                o_scr[pl.ds(r0, tt), hd * X_HEAD:(hd + 1) * X_HEAD] = o[hd * tt:(hd + 1) * tt, :]
            return carry

        lax.fori_loop(0, g, one_seq, 0)

    out = h + _dot(o_scr[...].astype(BF16), wco_ref[...])
    out_ref[...] = out.reshape(g, tt, d)


def _out_cross(x3, da3, lru3, mk3, mv3, mem_first, wo_da, wo_lru, gn, wcq, gcq, wco, g, tt):
    n_seq, t, d = x3.shape
    assert mk3.shape[-1] == X_HEAD or (mk3.shape[-1] == X_WIDTH and g == 1)
    mem_blk0 = mem_first // g
    blk = lambda w: pl.BlockSpec((g, tt, w), lambda s, i: (s, i, 0))
    mem = pl.BlockSpec((g,) + mk3.shape[1:], lambda s, i: (s + mem_blk0, 0, 0))
    return pl.pallas_call(
        functools.partial(_out_cross_kernel, g=g, tt=tt),
        grid=(n_seq // g, t // tt),
        in_specs=[blk(d), blk(DA_WIDTH), blk(LRU_WIDTH), mem, mem]
                 + [_const_spec(a.shape) for a in (wo_da, wo_lru, gn, wcq, gcq, wco)],
        out_specs=blk(d),
        out_shape=jax.ShapeDtypeStruct((n_seq, t, d), F32),
        scratch_shapes=[pltpu.VMEM((g * tt, X_WIDTH), F32), pltpu.VMEM((g * tt, X_WIDTH), F32)],
        compiler_params=_params(2),
        name="out_cross",
    )(x3, da3, lru3, mk3, mv3, wo_da, wo_lru, gn, wcq, gcq, wco)


def _conv_ffn_kernel(x_ref, prev_ref, gn_ref, wup_ref, cw_ref, cb_ref, wdn_ref, out_ref, state_ref,
                     halo_scr, ext_scr, *, g, tt, fc):
    i = pl.program_id(1)
    r = g * tt
    d = x_ref.shape[-1]
    f = wdn_ref.shape[0]
    hist = FFN_CONV - 1
    top = SUBLANES

    @pl.when(i == 0)
    def _():
        halo_scr[:, top - hist:top, :] = prev_ref[...]

    x = x_ref[...].reshape(r, d)
    xn = _rmsnorm(x, gn_ref[...]).astype(BF16)
    out_ref[...] = x_ref[...]

    for ch in range(f // fc):
        halves = []
        for half in range(2):
            c0 = half * f + ch * fc
            cols = slice(c0, c0 + fc)
            u3 = _dot(xn, wup_ref[:, cols]).reshape(g, tt, fc)
            ext_scr[:, top:top + tt, :] = u3
            ext_scr[:, top - hist:top, :] = halo_scr[:, top - hist:top, cols]
            c3 = cb_ref[:, cols]
            for j in range(FFN_CONV):
                c3 = c3 + ext_scr[:, top - hist + j:top - hist + j + tt, :] * cw_ref[j:j + 1, cols]
            halo_scr[:, top - hist:top, cols] = ext_scr[:, top + tt - hist:top + tt, :]
            halves.append(c3.reshape(r, fc))
        act = (halves[0] * _sigmoid(halves[0]) * halves[1]).astype(BF16)
        out_ref[...] += _dot(act, wdn_ref[ch * fc:(ch + 1) * fc, :]).reshape(g, tt, d)

    @pl.when(i == pl.num_programs(1) - 1)
    def _():
        state_ref[...] = halo_scr[:, top - hist:top, :]


def _conv_ffn(x3, prev, gn, wup, cw, cb, wdn, g, tt):
    n_seq, t, d = x3.shape
    f2 = wup.shape[1]
    fc = _tile(f2 // 2, FFN_CHUNK)
    blk = pl.BlockSpec((g, tt, d), lambda s, i: (s, i, 0))
    st = pl.BlockSpec((g, FFN_CONV - 1, f2), lambda s, i: (s, 0, 0))
    return pl.pallas_call(
        functools.partial(_conv_ffn_kernel, g=g, tt=tt, fc=fc),
        grid=(n_seq // g, t // tt),
        in_specs=[blk, st] + [_const_spec(a.shape) for a in (gn, wup, cw, cb, wdn)],
        out_specs=[blk, st],
        out_shape=[jax.ShapeDtypeStruct((n_seq, t, d), F32), jax.ShapeDtypeStruct((n_seq, FFN_CONV - 1, f2), F32)],
        scratch_shapes=[pltpu.VMEM((g, SUBLANES, f2), F32), pltpu.VMEM((g, SUBLANES + tt, fc), F32)],
        compiler_params=_params(2),
        name="conv_ffn",
    )(x3, prev, gn, wup, cw, cb, wdn)


def _block_diag(blocks):
    n, a, b = blocks.shape
    eye = jnp.eye(n, dtype=blocks.dtype)
    return (eye[:, None, :, None] * blocks[:, :, None, :]).reshape(n * a, n * b)


def _layer_weights(l, norm_mix, w_in, g_q, g_k, lam_q1, lam_k1, lam_q2, lam_k2, g_sub, lru_conv_w, lru_conv_b,
                   lru_wa, lru_ba, lru_wx, lru_bx, lru_lambda, w_o, norm_cross, norm_mem, w_cq, w_ck, w_cv,
                   g_cq, g_ck, w_co, norm_ffn, w_up, ffn_conv_w, ffn_conv_b, w_down):
    row = lambda a: a[l].reshape(1, -1)
    n_maps = DA_WIDTH // DA_HALF
    return dict(
        norm_mix=row(norm_mix), w_in=w_in[l].astype(BF16),
        pmat=jnp.kron(jnp.eye(n_maps, dtype=F32), jnp.full((DA_HALF, DA_HALF), 1.0 / DA_HALF, F32)).astype(BF16),
        gq=jnp.tile(g_q[l] * (DA_HALF ** -0.5), n_maps).reshape(1, -1),
        gq_base2=jnp.tile(g_q[l] * (DA_HALF ** -0.5 * LOG2E), n_maps).reshape(1, -1),
        gk=jnp.tile(g_k[l], n_maps).reshape(1, -1),
        lam=jnp.stack([lam_q1[l], lam_k1[l], lam_q2[l], lam_k2[l]]),
        gsub2=jnp.tile(g_sub[l], LANES // DA_HEAD).reshape(1, -1),
        gsub=row(g_sub),
        cw=lru_conv_w[l], cb=row(lru_conv_b),
        wa=_block_diag(lru_wa[l]).astype(BF16), ba=row(lru_ba),
        wx=_block_diag(lru_wx[l]).astype(BF16), bx=row(lru_bx), lru_lambda=row(lru_lambda),
        wo_da=w_o[l, :DA_WIDTH].astype(BF16), wo_lru=w_o[l, DA_WIDTH:].astype(BF16),
        norm_cross=row(norm_cross), norm_mem=row(norm_mem),
        wcq=w_cq[l].astype(BF16), wck=w_ck[l].astype(BF16), wcv=w_cv[l].astype(BF16),
        gcq=(g_cq[l] * (X_HEAD ** -0.5)).reshape(1, -1), gck=row(g_ck), wco=w_co[l].astype(BF16),
        norm_ffn=row(norm_ffn), wup=w_up[l].astype(BF16), fcw=ffn_conv_w[l], fcb=row(ffn_conv_b),
        wdn=w_down[l].astype(BF16),
    )


def _mix_and_ffn(x3, da3, lx, lg, conv_prev, h0, ffn_prev, mk3, mv3, mem_first, p, g, tt, mid_dtype):
    n_seq, t, d = x3.shape
    assert t >= LRU_CONV - 1 and t >= FFN_CONV - 1
    lru3, h_last = _lru(lx.reshape(n_seq, t, LRU_WIDTH), lg.reshape(n_seq, t, LRU_WIDTH), conv_prev, h0,
                        p['cw'], p['cb'], p['wa'], p['ba'], p['wx'], p['bx'], p['lru_lambda'], g, tt, mid_dtype)
    g_x = g
    while g_x > 1 and 4 * g_x * mk3.shape[1] * mk3.shape[2] * 4 > MEM_BLOCK_BYTES:
        g_x //= 2
    h3 = _out_cross(x3, da3, lru3, mk3, mv3, mem_first, p['wo_da'], p['wo_lru'], p['norm_cross'], p['wcq'], p['gcq'],
                    p['wco'], g_x, tt)
    y3, ffn_state = _conv_ffn(h3, ffn_prev, p['norm_ffn'], p['wup'], p['fcw'], p['fcb'], p['wdn'], g, tt)
    conv_state = lx.reshape(n_seq, t, LRU_WIDTH)[:, t - (LRU_CONV - 1):]
    return y3, conv_state, h_last.reshape(n_seq, LRU_WIDTH), ffn_state


def kernel(x_prompt, x_sample, mem_prompt, cache_k, cache_v, page_table, cache_mem_k, cache_mem_v, state_lru_conv, state_lru_h, state_ffn_conv, norm_mix, w_in, g_q, g_k, lam_q1, lam_k1, lam_q2, lam_k2, g_sub, lru_conv_w, lru_conv_b, lru_wa, lru_ba, lru_wx, lru_bx, lru_lambda, w_o, norm_cross, norm_mem, w_cq, w_ck, w_cv, g_cq, g_ck, w_co, norm_ffn, w_up, ffn_conv_w, ffn_conv_b, w_down):
    depth = w_in.shape[0]
    b, t, d = x_prompt.shape
    nb, nt, _ = x_sample.shape
    n_mem = mem_prompt.shape[1]
    n_phys, page = cache_k.shape[1], cache_k.shape[2]
    f2 = w_up.shape[2]
    to_pages = lambda c: jnp.transpose(c, (0, 1, 3, 4, 2)).reshape(depth * n_phys, N_HEADS_DA, DA_HEAD, page)
    cache_kt, cache_vt = to_pages(cache_k), to_pages(cache_v)
    to_heads = lambda a, n, m: jnp.transpose(a.reshape(n, m, N_HEADS_DA, DA_HEAD), (0, 2, 1, 3))
    cmk = cache_mem_k.reshape(depth * nb, n_mem * N_HEADS_X, X_HEAD)
    cmv = cache_mem_v.reshape(depth * nb, n_mem * N_HEADS_X, X_HEAD)

    tt_p = _tile(t, ROW_TILE)
    g_s = SAMPLE_SEQS if nb % SAMPLE_SEQS == 0 else 1

    yp, ys = x_prompt, x_sample
    outs_p, outs_s = [], []
    for l in range(depth):
        p = _layer_weights(l, norm_mix, w_in, g_q, g_k, lam_q1, lam_k1, lam_q2, lam_k2, g_sub, lru_conv_w,
                           lru_conv_b, lru_wa, lru_ba, lru_wx, lru_bx, lru_lambda, w_o, norm_cross, norm_mem,
                           w_cq, w_ck, w_cv, g_cq, g_ck, w_co, norm_ffn, w_up, ffn_conv_w, ffn_conv_b, w_down)
        lam_init = 0.8 - 0.6 * math.exp(-0.3 * l)

        q, kt, vt, lx, lg = _in_proj(yp.reshape(b * t, d), p['norm_mix'], p['w_in'], p['pmat'], p['gq_base2'], p['gk'],
                                     BF16, seq_len=t)
        da = _prompt_attn(q, kt, vt, p['lam'], p['gsub2'], lam_init)
        from_t = lambda a: jnp.transpose(a.reshape(b, N_HEADS_DA, DA_HEAD, t), (0, 3, 1, 2))
        mk, mv = _mem_kv(mem_prompt.reshape(b * n_mem, d), p['norm_mem'], p['wck'], p['wcv'], p['gck'])
        yp, conv_state, h_last, ffn_state = _mix_and_ffn(
            yp, da.reshape(b, t, DA_WIDTH), lx, lg,
            jnp.zeros((b, LRU_CONV - 1, LRU_WIDTH), F32), jnp.zeros((b, 1, LRU_WIDTH), F32),
            jnp.zeros((b, FFN_CONV - 1, f2), F32),
            mk.reshape(b, n_mem, X_WIDTH), mv.reshape(b, n_mem, X_WIDTH), 0, p, 1, tt_p, BF16)
        outs_p.append((from_t(kt), from_t(vt),
                       mk.reshape(b, n_mem, N_HEADS_X, X_HEAD), mv.reshape(b, n_mem, N_HEADS_X, X_HEAD),
                       conv_state, h_last, ffn_state))

        q, k, v, lx, lg = _in_proj(ys.reshape(nb * nt, d), p['norm_mix'], p['w_in'], p['pmat'], p['gq'], p['gk'], F32)
        page_ids = page_table.reshape(-1).astype(jnp.int32) + l * n_phys
        da = _sample_attn(page_ids, to_heads(q, nb, nt), to_heads(k, nb, nt), to_heads(v, nb, nt),
                          cache_kt, cache_vt, p['lam'], p['gsub'], lam_init)
        da = jnp.transpose(da, (0, 2, 1, 3))
        ys, conv_state, h_last, ffn_state = _mix_and_ffn(
            ys, da.reshape(nb, nt, DA_WIDTH), lx, lg,
            state_lru_conv[l], state_lru_h[l].reshape(nb, 1, LRU_WIDTH), state_ffn_conv[l],
            cmk, cmv, l * nb, p, g_s, nt, F32)
        outs_s.append((k.reshape(nb, nt, N_HEADS_DA, DA_HEAD), v.reshape(nb, nt, N_HEADS_DA, DA_HEAD),
                       conv_state, h_last, ffn_state))

    stack = lambda outs, j: jnp.stack([o[j] for o in outs])
    return (yp, ys, stack(outs_p, 0), stack(outs_p, 1), stack(outs_s, 0), stack(outs_s, 1),
            stack(outs_p, 2), stack(outs_p, 3), stack(outs_p, 4), stack(outs_s, 2),
            stack(outs_p, 5), stack(outs_s, 3), stack(outs_p, 6), stack(outs_s, 4))
```

```python
import functools
import math

import jax
import jax.numpy as jnp
from jax import lax
from jax.experimental import pallas as pl
from jax.experimental.pallas import tpu as pltpu

F32 = jnp.float32
BF16 = jnp.bfloat16

EPS = 1e-6
N_HEADS_DA = 8
DA_HEAD = 64
DA_HALF = DA_HEAD // 2
DA_WIDTH = N_HEADS_DA * DA_HEAD
LRU_WIDTH = 512
LRU_CONV = 4
LRU_C = 8.0
N_HEADS_X = 4
X_HEAD = 128
X_WIDTH = N_HEADS_X * X_HEAD
FFN_CONV = 3

LANES = 128
SUBLANES = 8
NEG = -1e30
VMEM_LIMIT = 48 * 1024 * 1024

ROW_TILE = 512
Q_TILE = 256
PAGES_PER_STEP = 16
FFN_CHUNK = 1408
FUSED_FFN_CHUNK = 256
FUSED_VMEM_LIMIT = 56 * 1024 * 1024
SAMPLE_SEQS = 32
MEM_BLOCK_BYTES = 16 * 1024 * 1024


def _tile(n, pref):
    t = min(n, pref)
    while n % t:
        t -= SUBLANES
    return t


def _const_spec(shape):
    zeros = (0,) * len(shape)
    return pl.BlockSpec(shape, lambda *_: zeros, pipeline_mode=pl.Buffered(1))


def _params(n_axes):
    return pltpu.CompilerParams(dimension_semantics=("arbitrary",) * n_axes, vmem_limit_bytes=VMEM_LIMIT)


def _rmsnorm(x, g):
    return x * lax.rsqrt(jnp.mean(x * x, axis=-1, keepdims=True) + EPS) * g


def _sigmoid(x):
    return 0.5 * jnp.tanh(0.5 * x) + 0.5


def _dot(a, b):
    return jnp.dot(a, b, preferred_element_type=F32)


def _dot_nt(a, b):
    return lax.dot_general(a, b, (((1,), (1,)), ((), ())), preferred_element_type=F32)


def _diff_lambda(lam_ref, lam_init):
    v = lam_ref[...]
    s1 = jnp.sum(v[0:1] * v[1:2], axis=-1, keepdims=True)
    s2 = jnp.sum(v[2:3] * v[3:4], axis=-1, keepdims=True)
    return jnp.exp(s1) - jnp.exp(s2) + lam_init


def _in_proj_kernel(x_ref, gn_ref, w_ref, p_ref, gq_ref, gk_ref, q_ref, k_ref, v_ref, lx_ref, lg_ref, *scratch):
    xn = _rmsnorm(x_ref[...], gn_ref[...]).astype(BF16)

    def proj(j):
        return _dot(xn, w_ref[:, j * DA_WIDTH:(j + 1) * DA_WIDTH])

    def map_norm(z, g):
        sq = z * z
        hi = sq.astype(BF16)
        lo = (sq - hi.astype(F32)).astype(BF16)
        ms = _dot(hi, p_ref[...]) + _dot(lo, p_ref[...])
        return z * lax.rsqrt(ms + EPS) * g

    q_ref[...] = map_norm(proj(0), gq_ref[...]).astype(q_ref.dtype)
    k = map_norm(proj(1), gk_ref[...])
    if scratch:
        v_scr, = scratch
        v_scr[...] = proj(2)
        k_ref[...] = k.T
        v_ref[...] = v_scr[...].T
    else:
        k_ref[...] = k
        v_ref[...] = proj(2)
    lx_ref[...] = proj(3)
    lg_ref[...] = proj(4)


def _in_proj(x2d, gn, w_in, pmat, gq, gk, q_dtype, seq_len=None):
    n, d = x2d.shape
    r = _tile(seq_len or n, ROW_TILE)
    row = lambda w: pl.BlockSpec((r, w), lambda i: (i, 0))
    out_sd = lambda dt: jax.ShapeDtypeStruct((n, DA_WIDTH), dt)
    kv_spec, kv_sd = row(DA_WIDTH), out_sd(F32)
    if seq_len:
        tiles = seq_len // r
        kv_spec = pl.BlockSpec((None, DA_WIDTH, r), lambda i: (i // tiles, 0, i % tiles))
        kv_sd = jax.ShapeDtypeStruct((n // seq_len, DA_WIDTH, seq_len), F32)
    return pl.pallas_call(
        _in_proj_kernel,
        grid=(n // r,),
        in_specs=[row(d), _const_spec(gn.shape), _const_spec(w_in.shape), _const_spec(pmat.shape),
                  _const_spec(gq.shape), _const_spec(gk.shape)],
        out_specs=[row(DA_WIDTH), kv_spec, kv_spec, row(DA_WIDTH), row(DA_WIDTH)],
        out_shape=[out_sd(q_dtype), kv_sd, kv_sd, out_sd(F32), out_sd(F32)],
        scratch_shapes=[pltpu.VMEM((r, DA_WIDTH), F32)] if seq_len else [],
        compiler_params=_params(1),
        name="in_proj",
    )(x2d, gn, w_in, pmat, gq, gk)


def _mem_kv_kernel(m_ref, gn_ref, wk_ref, wv_ref, gck_ref, k_ref, v_ref):
    mn = _rmsnorm(m_ref[...], gn_ref[...]).astype(BF16)
    k = _dot(mn, wk_ref[...])
    for h in range(N_HEADS_X):
        sl = slice(h * X_HEAD, (h + 1) * X_HEAD)
        k_ref[:, sl] = _rmsnorm(k[:, sl], gck_ref[...])
    v_ref[...] = _dot(mn, wv_ref[...])


def _mem_kv(mem2d, gn, w_ck, w_cv, g_ck):
    n, d = mem2d.shape
    r = _tile(n, ROW_TILE)
    row = lambda w: pl.BlockSpec((r, w), lambda i: (i, 0))
    return pl.pallas_call(
        _mem_kv_kernel,
        grid=(n // r,),
        in_specs=[row(d), _const_spec(gn.shape), _const_spec(w_ck.shape), _const_spec(w_cv.shape),
                  _const_spec(g_ck.shape)],
        out_specs=[row(X_WIDTH)] * 2,
        out_shape=[jax.ShapeDtypeStruct((n, X_WIDTH), F32)] * 2,
        compiler_params=_params(1),
        name="mem_kv",
    )(mem2d, gn, w_ck, w_cv, g_ck)


def _head_slope(head, shape):
    e = (head + 1).astype(F32) * (-8.0 / N_HEADS_DA)
    return jnp.exp2(jnp.full(shape, e, F32))


def _sub_norm_pair(o, lane, gsub, lam_init):
    sq = o * o
    first = lane < DA_HEAD
    ms0 = jnp.sum(jnp.where(first, sq, 0.0), axis=-1, keepdims=True)
    ms1 = jnp.sum(jnp.where(first, 0.0, sq), axis=-1, keepdims=True)
    ms = jnp.where(first, ms0, ms1) * (1.0 / DA_HEAD)
    return (o * lax.rsqrt(ms + EPS) * gsub) * (1.0 - lam_init)


BIAS_SPLIT = 256


N_BIAS_TERMS = 3
LOG2E = math.log2(math.e)


def _prompt_attn_kernel(lam_ref, gsub_ref, q_ref, kt_ref, vt_ref, o_ref, kb_scr, vb_scr, *, tq, lam_init):
    hp = pl.program_id(1)
    t = kt_ref.shape[1]
    other = lambda hh: (1 - hh) * DA_HEAD

    slab_row = lax.broadcasted_iota(jnp.int32, (LANES, t), 0)
    pos = lax.broadcasted_iota(jnp.int32, (LANES, t), 1)
    pos_hi = (pos // BIAS_SPLIT * BIAS_SPLIT).astype(F32)
    pos_lo = (pos % BIAS_SPLIT).astype(F32)
    for hh in range(2):
        own = (slab_row >= hh * DA_HEAD) & (slab_row < (hh + 1) * DA_HEAD)
        rel = slab_row - other(hh)
        in_bias = (rel >= 0) & (rel < 2 * N_BIAS_TERMS)
        bias = jnp.where(in_bias, jnp.where(rel % 2 == 0, pos_hi, pos_lo), 0.0)
        kb_scr[hh] = jnp.where(own, kt_ref[...], bias).astype(BF16)
        vb_scr[hh] = jnp.where(own, vt_ref[...], 1.0).astype(BF16)

    lane = lax.broadcasted_iota(jnp.int32, (tq, LANES), 1)
    visible = (lax.broadcasted_iota(jnp.int32, (tq, tq), 1) <= lax.broadcasted_iota(jnp.int32, (tq, tq), 0))
    lam = _diff_lambda(lam_ref, lam_init)

    def q_slab(q, hh, c):
        lo = hh * DA_HEAD + c * DA_HALF
        rel = lane - other(hh)
        rest = _head_slope(hp * 2 + hh, (1, LANES)) * LOG2E
        factor = jnp.zeros((tq, LANES), F32)
        for piece in range(N_BIAS_TERMS):
            part = rest.astype(BF16).astype(F32)
            factor = jnp.where((rel >= 2 * piece) & (rel < 2 * piece + 2), part, factor)
            rest = rest - part
        return jnp.where((lane >= lo) & (lane < lo + DA_HALF), q, factor).astype(BF16)

    for qi in range(t // tq):
        rows = slice(qi * tq, (qi + 1) * tq)
        past = qi * tq
        q = q_ref[rows, :].astype(F32)
        heads = []
        for hh in range(2):
            own = (lane >= hh * DA_HEAD) & (lane < (hh + 1) * DA_HEAD)
            maps = []
            for c in range(2):
                qm = q_slab(q, hh, c)
                s_diag = jnp.where(visible, _dot(qm, kb_scr[hh, :, rows]), NEG)
                m = jnp.max(s_diag, axis=-1, keepdims=True)
                if past:
                    s_past = _dot(qm, kb_scr[hh, :, :past])
                    m = jnp.maximum(m, jnp.max(s_past, axis=-1, keepdims=True))
                acc = _dot_nt(jnp.exp2(s_diag - m).astype(BF16), vb_scr[hh, :, rows])
                if past:
                    acc = acc + _dot_nt(jnp.exp2(s_past - m).astype(BF16), vb_scr[hh, :, :past])
                maps.append(acc / jnp.where(own, pltpu.roll(acc, DA_HEAD, 1), 1.0))
            heads.append(maps[0] - lam * maps[1])
        o = jnp.where(lane < DA_HEAD, heads[0], heads[1])
        o_ref[rows, :] = _sub_norm_pair(o, lane, gsub_ref[...], lam_init).astype(o_ref.dtype)


def _prompt_attn(q, kt, vt, lam_vecs, gsub2, lam_init):
    b, _, t = kt.shape
    tq = _tile(t, Q_TILE)
    n_pairs = DA_WIDTH // LANES
    assert t <= BIAS_SPLIT * 256
    spec = pl.BlockSpec((t, LANES), lambda bi, hp: (bi, hp))
    spec_t = pl.BlockSpec((None, LANES, t), lambda bi, hp: (bi, hp, 0))
    return pl.pallas_call(
        functools.partial(_prompt_attn_kernel, tq=tq, lam_init=lam_init),
        grid=(b, n_pairs),
        in_specs=[_const_spec(lam_vecs.shape), _const_spec(gsub2.shape), spec, spec_t, spec_t],
        out_specs=spec,
        out_shape=jax.ShapeDtypeStruct((b * t, DA_WIDTH), BF16),
        scratch_shapes=[pltpu.VMEM((2, LANES, t), BF16), pltpu.VMEM((2, LANES, t), BF16)],
        compiler_params=_params(2),
        name="prompt_attn",
    )(lam_vecs, gsub2, q, kt, vt)


def _ffn_unit(refs, n_units, n_ch, tiles_per_seq):
    (x_ref, prev_ref, gn_ref, wup_ref, cw_ref, cb_ref, wdn_ref, y_ref, state_ref, xn_scr, halo_scr, ext_scr) = refs
    hist = FFN_CONV - 1
    top = SUBLANES
    rows, d = x_ref.shape[1], x_ref.shape[2]
    u = pl.program_id(0) * pl.num_programs(1) + pl.program_id(1)
    active = u < n_units
    ua = jnp.minimum(u, n_units - 1)
    rt = ua // n_ch
    ch = ua % n_ch
    first_chunk = active & (ch == 0)

    @pl.when(first_chunk & (rt % tiles_per_seq == 0))
    def _():
        halo_scr[:, 0, top - hist:top, :] = prev_ref[0]

    @pl.when(first_chunk)
    def _():
        xn_scr[...] = _rmsnorm(x_ref[0], gn_ref[...]).astype(BF16)
        y_ref[...] = x_ref[...]

    halves = []

    def conv_half(half):
        j = half * n_ch + ch
        halves.append(_ffn_conv_half(xn_scr[...], 1, rows, wup_ref[j], cw_ref[j], cb_ref[j], halo_scr.at[j], ext_scr))

    def down():
        y_ref[...] += _dot(_ffn_gate(*halves), wdn_ref[ch]).reshape(1, rows, d)

    def finish():
        @pl.when(active & (rt % tiles_per_seq == tiles_per_seq - 1))
        def _():
            for h in range(2):
                state_ref[0, h * n_ch + ch] = halo_scr[h * n_ch + ch, 0, top - hist:top, :]

    return active, (functools.partial(conv_half, 0), functools.partial(conv_half, 1), down), finish


def _sample_attn_kernel(pt_ref, lam_ref, gsub_ref, q_ref, kn_ref, vn_ref, *refs, n_pp, t, past, lam_init, ffn):
    k_pages = refs[:n_pp]
    v_pages = refs[n_pp:2 * n_pp]
    refs = refs[2 * n_pp:]
    no_stages = (lambda: None,) * 3
    if ffn:
        o_ref = refs[7]
        q_scr, m_scr, l_scr, acc_scr = refs[10:14]
        ffn_active, ffn_stages, ffn_finish = _ffn_unit(refs[:7] + refs[8:10] + refs[14:], *ffn)
    else:
        o_ref, q_scr, m_scr, l_scr, acc_scr = refs
    del pt_ref
    ci = pl.program_id(1)
    rows_h = 2 * t
    n_rows = N_HEADS_DA * rows_h
    tk = n_pp * k_pages[0].shape[-1]
    head_rows = lambda h: slice(h * rows_h, (h + 1) * rows_h)

    @pl.when(ci == 0)
    def _():
        first_map = lax.broadcasted_iota(jnp.int32, (t, DA_HEAD), 1) < DA_HALF
        for h in range(N_HEADS_DA):
            qh = q_ref[h]
            q_scr[h * rows_h:h * rows_h + t, :] = jnp.where(first_map, qh, 0.0)
            q_scr[h * rows_h + t:(h + 1) * rows_h, :] = jnp.where(first_map, 0.0, qh)
        m_scr[...] = jnp.full(m_scr.shape, NEG, F32)
        l_scr[...] = jnp.zeros(l_scr.shape, F32)
        acc_scr[...] = jnp.zeros(acc_scr.shape, F32)

    r_idx = lax.broadcasted_iota(jnp.int32, (n_rows, 1), 0)
    slope = jnp.exp2((r_idx // rows_h + 1).astype(F32) * (-8.0 / N_HEADS_DA))
    q_pos = r_idx % t

    def scores(keys_t):
        return jnp.concatenate(
            [_dot(q_scr[head_rows(h), :].astype(BF16), keys_t(h)) for h in range(N_HEADS_DA)], axis=0)

    def update(s, weighted_values, between=lambda: None):
        m_old = m_scr[...]
        m_new = jnp.maximum(m_old, jnp.max(s, axis=-1, keepdims=True))
        alpha = jnp.exp(m_old - m_new)
        p = jnp.exp(s - m_new)
        l_scr[...] = alpha * l_scr[...] + jnp.sum(p, axis=-1, keepdims=True)
        pb = p.astype(BF16)
        between()
        pv = jnp.concatenate([weighted_values(h, pb[head_rows(h), :]) for h in range(N_HEADS_DA)], axis=0)
        acc_scr[...] = alpha * acc_scr[...] + pv
        m_scr[...] = m_new

    def cached_pages(stages):
        key_off = lax.broadcasted_iota(jnp.int32, (1, tk), 1) + (ci * tk - past)
        s = scores(lambda h: jnp.concatenate([kp[h] for kp in k_pages], axis=1).astype(BF16))
        s = s - slope * (q_pos - key_off).astype(F32)
        stages[0]()
        update(s, lambda h, ph: _dot_nt(ph, jnp.concatenate([vp[h] for vp in v_pages], axis=1).astype(BF16)),
               between=stages[1])
        stages[2]()

    if ffn:
        pl.when(ffn_active)(lambda: cached_pages(ffn_stages))
        pl.when(jnp.logical_not(ffn_active))(lambda: cached_pages(no_stages))
        ffn_finish()
    else:
        cached_pages(no_stages)

    @pl.when(ci == pl.num_programs(1) - 1)
    def _():
        pad = jnp.zeros((LANES - t, DA_HEAD), F32)
        padded = lambda ref, h: jnp.concatenate([ref[h], pad], axis=0).astype(BF16)
        new_i = lax.broadcasted_iota(jnp.int32, (1, LANES), 1)
        s = jnp.concatenate(
            [_dot_nt(q_scr[head_rows(h), :].astype(BF16), padded(kn_ref, h)) for h in range(N_HEADS_DA)], axis=0)
        s = s - slope * (q_pos - new_i).astype(F32)
        s = jnp.where((new_i < t) & (new_i <= q_pos), s, NEG)
        update(s, lambda h, ph: _dot(ph, padded(vn_ref, h)))

        o = acc_scr[...] / l_scr[...]
        lam = _diff_lambda(lam_ref, lam_init)
        for h in range(N_HEADS_DA):
            d = o[h * rows_h:h * rows_h + t, :] - lam * o[h * rows_h + t:(h + 1) * rows_h, :]
            o_ref[h] = _rmsnorm(d, gsub_ref[...]) * (1.0 - lam_init)


def _pages_per_step(n_pages):
    n_pp = min(PAGES_PER_STEP, n_pages)
    while n_pages % n_pp:
        n_pp -= 1
    return n_pp


def _ffn_units(x3, tt, f):
    n_ch = f // _tile(f, FUSED_FFN_CHUNK)
    return x3.shape[0] * (x3.shape[1] // tt) * n_ch, n_ch, x3.shape[1] // tt


def _sample_attn(page_ids, q4, k_new4, v_new4, cache_kt, cache_vt, lam_vecs, gsub, lam_init, ffn=None):
    n_seq, n_heads, t, dh = q4.shape
    page = cache_kt.shape[-1]
    n_pages = page_ids.shape[0] // n_seq
    n_pp = _pages_per_step(n_pages)
    n_rows = 2 * n_heads * t
    assert (2 * t) % 16 == 0 and t <= LANES

    def page_spec(i):
        return pl.BlockSpec((None, n_heads, dh, page),
                            lambda s, c, pt, i=i: (pt[s * n_pages + c * n_pp + i], 0, 0, 0))

    seq_spec = pl.BlockSpec((None, n_heads, t, dh), lambda s, c, pt: (s, 0, 0, 0))
    const = lambda shape: pl.BlockSpec(shape, lambda s, c, pt: (0,) * len(shape), pipeline_mode=pl.Buffered(1))
    n_chunks = n_pages // n_pp
    in_specs = ([const(lam_vecs.shape), const(gsub.shape), seq_spec, seq_spec, seq_spec]
                + [page_spec(i) for i in range(n_pp)] * 2)
    out_specs = [seq_spec]
    out_shape = [jax.ShapeDtypeStruct((n_seq, n_heads, t, dh), F32)]
    scratch = [pltpu.VMEM((n_rows, dh), F32), pltpu.VMEM((n_rows, 1), F32),
               pltpu.VMEM((n_rows, 1), F32), pltpu.VMEM((n_rows, dh), F32)]
    operands = [page_ids, lam_vecs, gsub, q4, k_new4, v_new4] + [cache_kt] * n_pp + [cache_vt] * n_pp
    ffn_static = None
    if ffn is not None:
        x3, prev, gn, wup, cw, cb, wdn, tt = ffn
        b, _, d = x3.shape
        f = wdn.shape[0]
        n_units, n_ch, tiles = _ffn_units(x3, tt, f)
        assert n_units <= n_seq * n_chunks
        fc = f // n_ch
        ffn_static = (n_units, n_ch, tiles)
        chunked = lambda a: jnp.transpose(a.reshape(a.shape[0], 2, n_ch, fc), (1, 2, 0, 3)).reshape(
            2 * n_ch, a.shape[0], fc)
        prev_c = jnp.transpose(prev.reshape(b, FFN_CONV - 1, 2 * n_ch, fc), (0, 2, 1, 3))

        def tile_of(s, c):
            return jnp.minimum(s * n_chunks + c, n_units - 1) // n_ch

        x_spec = pl.BlockSpec((1, tt, d), lambda s, c, pt: (tile_of(s, c) // tiles, tile_of(s, c) % tiles, 0))
        st_spec = pl.BlockSpec((1, 2 * n_ch, FFN_CONV - 1, fc), lambda s, c, pt: (tile_of(s, c) // tiles, 0, 0, 0))
        ffn_in = [x3, prev_c, gn, chunked(wup), chunked(cw), chunked(cb), wdn.reshape(n_ch, fc, d)]
        in_specs += [x_spec, st_spec] + [const(a.shape) for a in ffn_in[2:]]
        out_specs += [x_spec, st_spec]
        out_shape += [jax.ShapeDtypeStruct(x3.shape, F32), jax.ShapeDtypeStruct(prev_c.shape, F32)]
        scratch += [pltpu.VMEM((tt, d), BF16), pltpu.VMEM((2 * n_ch, 1, SUBLANES, fc), F32),
                    pltpu.VMEM((1, SUBLANES + tt, fc), F32)]
        operands += ffn_in
    grid_spec = pltpu.PrefetchScalarGridSpec(
        num_scalar_prefetch=1, grid=(n_seq, n_chunks), in_specs=in_specs, out_specs=out_specs, scratch_shapes=scratch)
    outs = pl.pallas_call(
        functools.partial(_sample_attn_kernel, n_pp=n_pp, t=t, past=n_pages * page, lam_init=lam_init,
                          ffn=ffn_static),
        grid_spec=grid_spec,
        out_shape=out_shape,
        compiler_params=pltpu.CompilerParams(dimension_semantics=("arbitrary",) * 2,
                                             vmem_limit_bytes=FUSED_VMEM_LIMIT if ffn is not None else VMEM_LIMIT),
        name="sample_attn",
    )(*operands)
    if ffn is None:
        return outs[0]
    state = jnp.transpose(outs[2], (0, 2, 1, 3)).reshape(prev.shape)
    return outs[0], outs[1], state


def _lru_kernel(lx_ref, lg_ref, cprev_ref, h0_ref, cw_ref, cb_ref, wa_ref, ba_ref, wx_ref, bx_ref, lam_ref,
                out_ref, hlast_ref, ext_scr, h_scr, a_scr, g_scr, *, g, tt):
    i = pl.program_id(1)
    w = LRU_WIDTH
    hist = LRU_CONV - 1
    top = SUBLANES

    @pl.when(i == 0)
    def _():
        ext_scr[:, top - hist:top, :] = cprev_ref[...]
        h_scr[...] = h0_ref[...]

    x3 = lx_ref[...]
    ext_scr[:, top:top + tt, :] = x3
    xc3 = cb_ref[...]
    for j in range(LRU_CONV):
        xc3 = xc3 + ext_scr[:, top - hist + j:top - hist + j + tt, :] * cw_ref[j:j + 1, :]
    ext_scr[:, top - hist:top, :] = ext_scr[:, top + tt - hist:top + tt, :]

    r = g * tt
    xc = xc3.reshape(r, w)
    xb = xc.astype(BF16)
    rg = _sigmoid(_dot(xb, wa_ref[...]) + ba_ref[...])
    ig = _sigmoid(_dot(xb, wx_ref[...]) + bx_ref[...])
    nl = -lam_ref[...]
    softplus = jnp.maximum(nl, 0.0) + jnp.log1p(jnp.exp(-jnp.abs(nl)))
    log_a = -LRU_C * rg * softplus
    a = jnp.exp(log_a)
    gx = jnp.sqrt(-jnp.tanh(log_a) * (a * a + 1.0)) * (ig * xc)

    nch = r // SUBLANES
    a3 = a.reshape(nch, SUBLANES, w)
    g3 = gx.reshape(nch, SUBLANES, w)
    sub = lax.broadcasted_iota(jnp.int32, (nch, SUBLANES, w), 1)
    for s in (1, 2, 4):
        keep = sub >= s
        a_prev = jnp.where(keep, pltpu.roll(a3, s, 1), 1.0)
        g_prev = jnp.where(keep, pltpu.roll(g3, s, 1), 0.0)
        g3 = a3 * g_prev + g3
        a3 = a3 * a_prev

    nc = tt // SUBLANES
    a_scr[...] = a3.reshape(g, nc, SUBLANES, w)
    g_scr[...] = g3.reshape(g, nc, SUBLANES, w)

    def chunk(c, h):
        hs = a_scr[:, c] * h + g_scr[:, c]
        g_scr[:, c] = hs
        return hs[:, SUBLANES - 1:SUBLANES, :]

    h = h_scr[...]
    if nc <= 2:
        for c in range(nc):
            h = chunk(c, h)
    else:
        h = lax.fori_loop(0, nc, chunk, h)
    h_scr[...] = h

    hs = g_scr[...].reshape(r, w)
    out = hs * jax.nn.gelu(lg_ref[...].reshape(r, w))
    out_ref[...] = out.reshape(g, tt, w).astype(out_ref.dtype)

    @pl.when(i == pl.num_programs(1) - 1)
    def _():
        hlast_ref[...] = h


def _lru(lx3, lg3, conv_prev, h0, cw, cb, wa, ba, wx, bx, lam, g, tt, out_dtype):
    n_seq, t, w = lx3.shape
    blk = pl.BlockSpec((g, tt, w), lambda s, i: (s, i, 0))
    seq = lambda rows: pl.BlockSpec((g, rows, w), lambda s, i: (s, 0, 0))
    return pl.pallas_call(
        functools.partial(_lru_kernel, g=g, tt=tt),
        grid=(n_seq // g, t // tt),
        in_specs=[blk, blk, seq(LRU_CONV - 1), seq(1)] + [_const_spec(a.shape) for a in (cw, cb, wa, ba, wx, bx, lam)],
        out_specs=[blk, seq(1)],
        out_shape=[jax.ShapeDtypeStruct((n_seq, t, w), out_dtype), jax.ShapeDtypeStruct((n_seq, 1, w), F32)],
        scratch_shapes=[pltpu.VMEM((g, SUBLANES + tt, w), F32), pltpu.VMEM((g, 1, w), F32),
                        pltpu.VMEM((g, tt // SUBLANES, SUBLANES, w), F32),
                        pltpu.VMEM((g, tt // SUBLANES, SUBLANES, w), F32)],
        compiler_params=_params(2),
        name="lru",
    )(lx3, lg3, conv_prev, h0, cw, cb, wa, ba, wx, bx, lam)


def _out_cross_kernel(x_ref, da_ref, lru_ref, mk_ref, mv_ref, wo_da_ref, wo_lru_ref, gn_ref, wcq_ref, gcq_ref,
                      wco_ref, out_ref, q_scr, o_scr, *, g, tt):
    r = g * tt
    d = x_ref.shape[-1]
    h = (x_ref[...].reshape(r, d)
         + _dot(da_ref[...].reshape(r, DA_WIDTH).astype(BF16), wo_da_ref[...])
         + _dot(lru_ref[...].reshape(r, LRU_WIDTH).astype(BF16), wo_lru_ref[...]))
    hn = _rmsnorm(h, gn_ref[...]).astype(BF16)
    qc = _dot(hn, wcq_ref[...])
    for hd in range(N_HEADS_X):
        sl = slice(hd * X_HEAD, (hd + 1) * X_HEAD)
        q_scr[:, sl] = _rmsnorm(qc[:, sl], gcq_ref[...])

    if mk_ref.shape[-1] == X_WIDTH:
        for hd in range(N_HEADS_X):
            sl = slice(hd * X_HEAD, (hd + 1) * X_HEAD)
            s = _dot_nt(q_scr[:, sl].astype(BF16), mk_ref[0, :, sl].astype(BF16))
            e = jnp.exp(s - jnp.max(s, axis=-1, keepdims=True))
            o = _dot(e.astype(BF16), mv_ref[0, :, sl].astype(BF16))
            o_scr[:, sl] = o / jnp.sum(e, axis=-1, keepdims=True)
    else:
        n_rows = N_HEADS_X * tt
        n_cols = mk_ref.shape[1]
        col_head = lax.broadcasted_iota(jnp.int32, (n_rows, n_cols), 1) % N_HEADS_X
        own_head = col_head == lax.broadcasted_iota(jnp.int32, (n_rows, n_cols), 0) // tt

        def one_seq(si, carry):
            r0 = pl.multiple_of(si * tt, tt)
            q = q_scr[pl.ds(r0, tt), :]
            qs = jnp.concatenate([q[:, hd * X_HEAD:(hd + 1) * X_HEAD] for hd in range(N_HEADS_X)], axis=0)
            s = jnp.where(own_head, _dot_nt(qs.astype(BF16), mk_ref[si].astype(BF16)), NEG)
            e = jnp.exp(s - jnp.max(s, axis=-1, keepdims=True))
            o = _dot(e.astype(BF16), mv_ref[si].astype(BF16)) / jnp.sum(e, axis=-1, keepdims=True)
            for hd in range(N_HEADS_X):
                o_scr[pl.ds(r0, tt), hd * X_HEAD:(hd + 1) * X_HEAD] = o[hd * tt:(hd + 1) * tt, :]
            return carry

        lax.fori_loop(0, g, one_seq, 0)

    out = h + _dot(o_scr[...].astype(BF16), wco_ref[...])
    out_ref[...] = out.reshape(g, tt, d)


def _out_cross(x3, da3, lru3, mk3, mv3, mem_first, wo_da, wo_lru, gn, wcq, gcq, wco, g, tt):
    n_seq, t, d = x3.shape
    assert mk3.shape[-1] == X_HEAD or (mk3.shape[-1] == X_WIDTH and g == 1)
    mem_blk0 = mem_first // g
    blk = lambda w: pl.BlockSpec((g, tt, w), lambda s, i: (s, i, 0))
    mem = pl.BlockSpec((g,) + mk3.shape[1:], lambda s, i: (s + mem_blk0, 0, 0))
    return pl.pallas_call(
        functools.partial(_out_cross_kernel, g=g, tt=tt),
        grid=(n_seq // g, t // tt),
        in_specs=[blk(d), blk(DA_WIDTH), blk(LRU_WIDTH), mem, mem]
                 + [_const_spec(a.shape) for a in (wo_da, wo_lru, gn, wcq, gcq, wco)],
        out_specs=blk(d),
        out_shape=jax.ShapeDtypeStruct((n_seq, t, d), F32),
        scratch_shapes=[pltpu.VMEM((g * tt, X_WIDTH), F32), pltpu.VMEM((g * tt, X_WIDTH), F32)],
        compiler_params=_params(2),
        name="out_cross",
    )(x3, da3, lru3, mk3, mv3, wo_da, wo_lru, gn, wcq, gcq, wco)


def _ffn_conv_half(xn, g, tt, up_w, conv_w, conv_b, h_ref, ext_scr):
    hist = FFN_CONV - 1
    top = SUBLANES
    fc = ext_scr.shape[-1]
    ext_scr[:, top:top + tt, :] = _dot(xn, up_w).reshape(g, tt, fc)
    ext_scr[:, top - hist:top, :] = h_ref[:, top - hist:top, :]
    c3 = conv_b
    for j in range(FFN_CONV):
        c3 = c3 + ext_scr[:, top - hist + j:top - hist + j + tt, :] * conv_w[j:j + 1, :]
    h_ref[:, top - hist:top, :] = ext_scr[:, top + tt - hist:top + tt, :]
    return c3.reshape(g * tt, fc)


def _ffn_gate(gate, value):
    return (gate * _sigmoid(gate) * value).astype(BF16)


def _conv_ffn_kernel(x_ref, prev_ref, gn_ref, wup_ref, cw_ref, cb_ref, wdn_ref, out_ref, state_ref,
                     halo_scr, ext_scr, *, g, tt, fc):
    i = pl.program_id(1)
    r = g * tt
    d = x_ref.shape[-1]
    f = wdn_ref.shape[0]
    hist = FFN_CONV - 1
    top = SUBLANES

    @pl.when(i == 0)
    def _():
        halo_scr[:, top - hist:top, :] = prev_ref[...]

    xn = _rmsnorm(x_ref[...].reshape(r, d), gn_ref[...]).astype(BF16)
    out_ref[...] = x_ref[...]

    for ch in range(f // fc):
        halves = []
        for half in range(2):
            cols = slice(half * f + ch * fc, half * f + (ch + 1) * fc)
            halves.append(_ffn_conv_half(xn, g, tt, wup_ref[:, cols], cw_ref[:, cols], cb_ref[:, cols],
                                         halo_scr.at[:, :, cols], ext_scr))
        out_ref[...] += _dot(_ffn_gate(*halves), wdn_ref[ch * fc:(ch + 1) * fc, :]).reshape(g, tt, d)

    @pl.when(i == pl.num_programs(1) - 1)
    def _():
        state_ref[...] = halo_scr[:, top - hist:top, :]


def _conv_ffn(x3, prev, gn, wup, cw, cb, wdn, g, tt):
    n_seq, t, d = x3.shape
    f2 = wup.shape[1]
    fc = _tile(f2 // 2, FFN_CHUNK)
    blk = pl.BlockSpec((g, tt, d), lambda s, i: (s, i, 0))
    st = pl.BlockSpec((g, FFN_CONV - 1, f2), lambda s, i: (s, 0, 0))
    return pl.pallas_call(
        functools.partial(_conv_ffn_kernel, g=g, tt=tt, fc=fc),
        grid=(n_seq // g, t // tt),
        in_specs=[blk, st] + [_const_spec(a.shape) for a in (gn, wup, cw, cb, wdn)],
        out_specs=[blk, st],
        out_shape=[jax.ShapeDtypeStruct((n_seq, t, d), F32), jax.ShapeDtypeStruct((n_seq, FFN_CONV - 1, f2), F32)],
        scratch_shapes=[pltpu.VMEM((g, SUBLANES, f2), F32), pltpu.VMEM((g, SUBLANES + tt, fc), F32)],
        compiler_params=_params(2),
        name="conv_ffn",
    )(x3, prev, gn, wup, cw, cb, wdn)


def _block_diag(blocks):
    n, a, b = blocks.shape
    eye = jnp.eye(n, dtype=blocks.dtype)
    return (eye[:, None, :, None] * blocks[:, :, None, :]).reshape(n * a, n * b)


def _layer_weights(l, norm_mix, w_in, g_q, g_k, lam_q1, lam_k1, lam_q2, lam_k2, g_sub, lru_conv_w, lru_conv_b,
                   lru_wa, lru_ba, lru_wx, lru_bx, lru_lambda, w_o, norm_cross, norm_mem, w_cq, w_ck, w_cv,
                   g_cq, g_ck, w_co, norm_ffn, w_up, ffn_conv_w, ffn_conv_b, w_down):
    row = lambda a: a[l].reshape(1, -1)
    n_maps = DA_WIDTH // DA_HALF
    return dict(
        norm_mix=row(norm_mix), w_in=w_in[l].astype(BF16),
        pmat=jnp.kron(jnp.eye(n_maps, dtype=F32), jnp.full((DA_HALF, DA_HALF), 1.0 / DA_HALF, F32)).astype(BF16),
        gq=jnp.tile(g_q[l] * (DA_HALF ** -0.5), n_maps).reshape(1, -1),
        gq_base2=jnp.tile(g_q[l] * (DA_HALF ** -0.5 * LOG2E), n_maps).reshape(1, -1),
        gk=jnp.tile(g_k[l], n_maps).reshape(1, -1),
        lam=jnp.stack([lam_q1[l], lam_k1[l], lam_q2[l], lam_k2[l]]),
        gsub2=jnp.tile(g_sub[l], LANES // DA_HEAD).reshape(1, -1),
        gsub=row(g_sub),
        cw=lru_conv_w[l], cb=row(lru_conv_b),
        wa=_block_diag(lru_wa[l]).astype(BF16), ba=row(lru_ba),
        wx=_block_diag(lru_wx[l]).astype(BF16), bx=row(lru_bx), lru_lambda=row(lru_lambda),
        wo_da=w_o[l, :DA_WIDTH].astype(BF16), wo_lru=w_o[l, DA_WIDTH:].astype(BF16),
        norm_cross=row(norm_cross), norm_mem=row(norm_mem),
        wcq=w_cq[l].astype(BF16), wck=w_ck[l].astype(BF16), wcv=w_cv[l].astype(BF16),
        gcq=(g_cq[l] * (X_HEAD ** -0.5)).reshape(1, -1), gck=row(g_ck), wco=w_co[l].astype(BF16),
        norm_ffn=row(norm_ffn), wup=w_up[l].astype(BF16), fcw=ffn_conv_w[l], fcb=row(ffn_conv_b),
        wdn=w_down[l].astype(BF16),
    )


def _mix(x3, da3, lx, lg, conv_prev, h0, mk3, mv3, mem_first, p, g, tt, mid_dtype):
    n_seq, t, d = x3.shape
    assert t >= LRU_CONV - 1 and t >= FFN_CONV - 1
    lru3, h_last = _lru(lx.reshape(n_seq, t, LRU_WIDTH), lg.reshape(n_seq, t, LRU_WIDTH), conv_prev, h0,
                        p['cw'], p['cb'], p['wa'], p['ba'], p['wx'], p['bx'], p['lru_lambda'], g, tt, mid_dtype)
    g_x = g
    while g_x > 1 and 4 * g_x * mk3.shape[1] * mk3.shape[2] * 4 > MEM_BLOCK_BYTES:
        g_x //= 2
    h3 = _out_cross(x3, da3, lru3, mk3, mv3, mem_first, p['wo_da'], p['wo_lru'], p['norm_cross'], p['wcq'], p['gcq'],
                    p['wco'], g_x, tt)
    conv_state = lx.reshape(n_seq, t, LRU_WIDTH)[:, t - (LRU_CONV - 1):]
    return h3, conv_state, h_last.reshape(n_seq, LRU_WIDTH)


def kernel(x_prompt, x_sample, mem_prompt, cache_k, cache_v, page_table, cache_mem_k, cache_mem_v, state_lru_conv, state_lru_h, state_ffn_conv, norm_mix, w_in, g_q, g_k, lam_q1, lam_k1, lam_q2, lam_k2, g_sub, lru_conv_w, lru_conv_b, lru_wa, lru_ba, lru_wx, lru_bx, lru_lambda, w_o, norm_cross, norm_mem, w_cq, w_ck, w_cv, g_cq, g_ck, w_co, norm_ffn, w_up, ffn_conv_w, ffn_conv_b, w_down):
    depth = w_in.shape[0]
    b, t, d = x_prompt.shape
    nb, nt, _ = x_sample.shape
    n_mem = mem_prompt.shape[1]
    n_phys, page = cache_k.shape[1], cache_k.shape[2]
    f2 = w_up.shape[2]
    to_pages = lambda c: jnp.transpose(c, (0, 1, 3, 4, 2)).reshape(depth * n_phys, N_HEADS_DA, DA_HEAD, page)
    cache_kt, cache_vt = to_pages(cache_k), to_pages(cache_v)
    to_heads = lambda a, n, m: jnp.transpose(a.reshape(n, m, N_HEADS_DA, DA_HEAD), (0, 2, 1, 3))
    cmk = cache_mem_k.reshape(depth * nb, n_mem * N_HEADS_X, X_HEAD)
    cmv = cache_mem_v.reshape(depth * nb, n_mem * N_HEADS_X, X_HEAD)

    tt_p = _tile(t, ROW_TILE)
    g_s = SAMPLE_SEQS if nb % SAMPLE_SEQS == 0 else 1

    yp, ys = x_prompt, x_sample
    outs_p, outs_s = [], []
    for l in range(depth):
        p = _layer_weights(l, norm_mix, w_in, g_q, g_k, lam_q1, lam_k1, lam_q2, lam_k2, g_sub, lru_conv_w,
                           lru_conv_b, lru_wa, lru_ba, lru_wx, lru_bx, lru_lambda, w_o, norm_cross, norm_mem,
                           w_cq, w_ck, w_cv, g_cq, g_ck, w_co, norm_ffn, w_up, ffn_conv_w, ffn_conv_b, w_down)
        lam_init = 0.8 - 0.6 * math.exp(-0.3 * l)

        q, kt, vt, lx, lg = _in_proj(yp.reshape(b * t, d), p['norm_mix'], p['w_in'], p['pmat'], p['gq_base2'], p['gk'],
                                     BF16, seq_len=t)
        da = _prompt_attn(q, kt, vt, p['lam'], p['gsub2'], lam_init)
        from_t = lambda a: jnp.transpose(a.reshape(b, N_HEADS_DA, DA_HEAD, t), (0, 3, 1, 2))
        mk, mv = _mem_kv(mem_prompt.reshape(b * n_mem, d), p['norm_mem'], p['wck'], p['wcv'], p['gck'])
        hp, conv_state_p, h_last_p = _mix(
            yp, da.reshape(b, t, DA_WIDTH), lx, lg,
            jnp.zeros((b, LRU_CONV - 1, LRU_WIDTH), F32), jnp.zeros((b, 1, LRU_WIDTH), F32),
            mk.reshape(b, n_mem, X_WIDTH), mv.reshape(b, n_mem, X_WIDTH), 0, p, 1, tt_p, BF16)
        ffn_p = (hp, jnp.zeros((b, FFN_CONV - 1, f2), F32), p['norm_ffn'], p['wup'], p['fcw'], p['fcb'], p['wdn'])

        q, k, v, lx, lg = _in_proj(ys.reshape(nb * nt, d), p['norm_mix'], p['w_in'], p['pmat'], p['gq'], p['gk'], F32)
        page_ids = page_table.reshape(-1).astype(jnp.int32) + l * n_phys
        attn_args = (page_ids, to_heads(q, nb, nt), to_heads(k, nb, nt), to_heads(v, nb, nt),
                     cache_kt, cache_vt, p['lam'], p['gsub'], lam_init)
        n_pages = page_table.shape[1]
        if _ffn_units(hp, tt_p, f2 // 2)[0] <= nb * (n_pages // _pages_per_step(n_pages)):
            da, yp, ffn_state_p = _sample_attn(*attn_args, ffn=ffn_p + (tt_p,))
        else:
            da = _sample_attn(*attn_args)
            yp, ffn_state_p = _conv_ffn(*ffn_p, 1, tt_p)
        outs_p.append((from_t(kt), from_t(vt),
                       mk.reshape(b, n_mem, N_HEADS_X, X_HEAD), mv.reshape(b, n_mem, N_HEADS_X, X_HEAD),
                       conv_state_p, h_last_p, ffn_state_p))
        da = jnp.transpose(da, (0, 2, 1, 3))
        hs, conv_state, h_last = _mix(
            ys, da.reshape(nb, nt, DA_WIDTH), lx, lg,
            state_lru_conv[l], state_lru_h[l].reshape(nb, 1, LRU_WIDTH),
            cmk, cmv, l * nb, p, g_s, nt, F32)
        ys, ffn_state = _conv_ffn(hs, state_ffn_conv[l], p['norm_ffn'], p['wup'], p['fcw'], p['fcb'], p['wdn'], g_s, nt)
        outs_s.append((k.reshape(nb, nt, N_HEADS_DA, DA_HEAD), v.reshape(nb, nt, N_HEADS_DA, DA_HEAD),
                       conv_state, h_last, ffn_state))

    stack = lambda outs, j: jnp.stack([o[j] for o in outs])
    return (yp, ys, stack(outs_p, 0), stack(outs_p, 1), stack(outs_s, 0), stack(outs_s, 1),
            stack(outs_p, 2), stack(outs_p, 3), stack(outs_p, 4), stack(outs_s, 2),
            stack(outs_p, 5), stack(outs_s, 3), stack(outs_p, 6), stack(outs_s, 4))
```

```python
import functools
import math

import jax
import jax.numpy as jnp
from jax import lax
from jax.experimental import pallas as pl
from jax.experimental.pallas import tpu as pltpu

F32 = jnp.float32
BF16 = jnp.bfloat16

EPS = 1e-6
N_HEADS_DA = 8
DA_HEAD = 64
DA_HALF = DA_HEAD // 2
DA_WIDTH = N_HEADS_DA * DA_HEAD
LRU_WIDTH = 512
LRU_CONV = 4
LRU_C = 8.0
N_HEADS_X = 4
X_HEAD = 128
X_WIDTH = N_HEADS_X * X_HEAD
FFN_CONV = 3

LANES = 128
SUBLANES = 8
NEG = -1e30
VMEM_LIMIT = 48 * 1024 * 1024

ROW_TILE = 512
Q_TILE = 256
PAGES_PER_STEP = 16
FFN_CHUNK = 1408
FUSED_FFN_CHUNK = 256
FUSED_VMEM_LIMIT = 56 * 1024 * 1024
SAMPLE_SEQS = 32
MEM_BLOCK_BYTES = 16 * 1024 * 1024


def _tile(n, pref):
    t = min(n, pref)
    while n % t:
        t -= SUBLANES
    return t


def _const_spec(shape):
    zeros = (0,) * len(shape)
    return pl.BlockSpec(shape, lambda *_: zeros, pipeline_mode=pl.Buffered(1))


def _params(n_axes):
    return pltpu.CompilerParams(dimension_semantics=("arbitrary",) * n_axes, vmem_limit_bytes=VMEM_LIMIT)


def _rmsnorm(x, g):
    return x * lax.rsqrt(jnp.mean(x * x, axis=-1, keepdims=True) + EPS) * g


def _sigmoid(x):
    return 0.5 * jnp.tanh(0.5 * x) + 0.5


def _dot(a, b):
    return jnp.dot(a, b, preferred_element_type=F32)


def _dot_nt(a, b):
    return lax.dot_general(a, b, (((1,), (1,)), ((), ())), preferred_element_type=F32)


def _diff_lambda(lam_ref, lam_init):
    v = lam_ref[...]
    s1 = jnp.sum(v[0:1] * v[1:2], axis=-1, keepdims=True)
    s2 = jnp.sum(v[2:3] * v[3:4], axis=-1, keepdims=True)
    return jnp.exp(s1) - jnp.exp(s2) + lam_init


def _in_proj_kernel(x_ref, gn_ref, w_ref, p_ref, gq_ref, gk_ref, q_ref, k_ref, v_ref, lx_ref, lg_ref, *scratch):
    xn = _rmsnorm(x_ref[...], gn_ref[...]).astype(BF16)

    def proj(j):
        return _dot(xn, w_ref[:, j * DA_WIDTH:(j + 1) * DA_WIDTH])

    def map_norm(z, g):
        sq = z * z
        hi = sq.astype(BF16)
        lo = (sq - hi.astype(F32)).astype(BF16)
        ms = _dot(hi, p_ref[...]) + _dot(lo, p_ref[...])
        return z * lax.rsqrt(ms + EPS) * g

    q_ref[...] = map_norm(proj(0), gq_ref[...]).astype(q_ref.dtype)
    k = map_norm(proj(1), gk_ref[...])
    if scratch:
        v_scr, = scratch
        v_scr[...] = proj(2)
        k_ref[...] = k.T
        v_ref[...] = v_scr[...].T
    else:
        k_ref[...] = k
        v_ref[...] = proj(2)
    lx_ref[...] = proj(3)
    lg_ref[...] = proj(4)


def _in_proj(x2d, gn, w_in, pmat, gq, gk, q_dtype, seq_len=None):
    n, d = x2d.shape
    r = _tile(seq_len or n, ROW_TILE)
    row = lambda w: pl.BlockSpec((r, w), lambda i: (i, 0))
    out_sd = lambda dt: jax.ShapeDtypeStruct((n, DA_WIDTH), dt)
    kv_spec, kv_sd = row(DA_WIDTH), out_sd(F32)
    if seq_len:
        tiles = seq_len // r
        kv_spec = pl.BlockSpec((None, DA_WIDTH, r), lambda i: (i // tiles, 0, i % tiles))
        kv_sd = jax.ShapeDtypeStruct((n // seq_len, DA_WIDTH, seq_len), F32)
    return pl.pallas_call(
        _in_proj_kernel,
        grid=(n // r,),
        in_specs=[row(d), _const_spec(gn.shape), _const_spec(w_in.shape), _const_spec(pmat.shape),
                  _const_spec(gq.shape), _const_spec(gk.shape)],
        out_specs=[row(DA_WIDTH), kv_spec, kv_spec, row(DA_WIDTH), row(DA_WIDTH)],
        out_shape=[out_sd(q_dtype), kv_sd, kv_sd, out_sd(F32), out_sd(F32)],
        scratch_shapes=[pltpu.VMEM((r, DA_WIDTH), F32)] if seq_len else [],
        compiler_params=_params(1),
        name="in_proj",
    )(x2d, gn, w_in, pmat, gq, gk)


def _mem_kv_kernel(m_ref, gn_ref, wk_ref, wv_ref, gck_ref, k_ref, v_ref):
    mn = _rmsnorm(m_ref[...], gn_ref[...]).astype(BF16)
    k = _dot(mn, wk_ref[...])
    for h in range(N_HEADS_X):
        sl = slice(h * X_HEAD, (h + 1) * X_HEAD)
        k_ref[:, sl] = _rmsnorm(k[:, sl], gck_ref[...])
    v_ref[...] = _dot(mn, wv_ref[...])


def _mem_kv(mem2d, gn, w_ck, w_cv, g_ck):
    n, d = mem2d.shape
    r = _tile(n, ROW_TILE)
    row = lambda w: pl.BlockSpec((r, w), lambda i: (i, 0))
    return pl.pallas_call(
        _mem_kv_kernel,
        grid=(n // r,),
        in_specs=[row(d), _const_spec(gn.shape), _const_spec(w_ck.shape), _const_spec(w_cv.shape),
                  _const_spec(g_ck.shape)],
        out_specs=[row(X_WIDTH)] * 2,
        out_shape=[jax.ShapeDtypeStruct((n, X_WIDTH), F32)] * 2,
        compiler_params=_params(1),
        name="mem_kv",
    )(mem2d, gn, w_ck, w_cv, g_ck)


def _head_slope(head, shape):
    e = (head + 1).astype(F32) * (-8.0 / N_HEADS_DA)
    return jnp.exp2(jnp.full(shape, e, F32))


def _sub_norm_pair(o, lane, gsub, lam_init):
    sq = o * o
    first = lane < DA_HEAD
    ms0 = jnp.sum(jnp.where(first, sq, 0.0), axis=-1, keepdims=True)
    ms1 = jnp.sum(jnp.where(first, 0.0, sq), axis=-1, keepdims=True)
    ms = jnp.where(first, ms0, ms1) * (1.0 / DA_HEAD)
    return (o * lax.rsqrt(ms + EPS) * gsub) * (1.0 - lam_init)


BIAS_SPLIT = 256


N_BIAS_TERMS = 3
LOG2E = math.log2(math.e)


def _prompt_attn_kernel(lam_ref, gsub_ref, q_ref, kt_ref, vt_ref, o_ref, kb_scr, vb_scr, *, tq, lam_init):
    hp = pl.program_id(1)
    t = kt_ref.shape[1]
    other = lambda hh: (1 - hh) * DA_HEAD

    slab_row = lax.broadcasted_iota(jnp.int32, (LANES, t), 0)
    pos = lax.broadcasted_iota(jnp.int32, (LANES, t), 1)
    pos_hi = (pos // BIAS_SPLIT * BIAS_SPLIT).astype(F32)
    pos_lo = (pos % BIAS_SPLIT).astype(F32)
    for hh in range(2):
        own = (slab_row >= hh * DA_HEAD) & (slab_row < (hh + 1) * DA_HEAD)
        rel = slab_row - other(hh)
        in_bias = (rel >= 0) & (rel < 2 * N_BIAS_TERMS)
        bias = jnp.where(in_bias, jnp.where(rel % 2 == 0, pos_hi, pos_lo), 0.0)
        kb_scr[hh] = jnp.where(own, kt_ref[...], bias).astype(BF16)
        vb_scr[hh] = jnp.where(own, vt_ref[...], 1.0).astype(BF16)

    lane = lax.broadcasted_iota(jnp.int32, (tq, LANES), 1)
    visible = (lax.broadcasted_iota(jnp.int32, (tq, tq), 1) <= lax.broadcasted_iota(jnp.int32, (tq, tq), 0))
    lam = _diff_lambda(lam_ref, lam_init)

    def q_slab(q, hh, c):
        lo = hh * DA_HEAD + c * DA_HALF
        rel = lane - other(hh)
        rest = _head_slope(hp * 2 + hh, (1, LANES)) * LOG2E
        factor = jnp.zeros((tq, LANES), F32)
        for piece in range(N_BIAS_TERMS):
            part = rest.astype(BF16).astype(F32)
            factor = jnp.where((rel >= 2 * piece) & (rel < 2 * piece + 2), part, factor)
            rest = rest - part
        return jnp.where((lane >= lo) & (lane < lo + DA_HALF), q, factor).astype(BF16)

    for qi in range(t // tq):
        rows = slice(qi * tq, (qi + 1) * tq)
        past = qi * tq
        q = q_ref[rows, :].astype(F32)
        heads = []
        for hh in range(2):
            own = (lane >= hh * DA_HEAD) & (lane < (hh + 1) * DA_HEAD)
            maps = []
            for c in range(2):
                qm = q_slab(q, hh, c)
                s_diag = jnp.where(visible, _dot(qm, kb_scr[hh, :, rows]), NEG)
                m = jnp.max(s_diag, axis=-1, keepdims=True)
                if past:
                    s_past = _dot(qm, kb_scr[hh, :, :past])
                    m = jnp.maximum(m, jnp.max(s_past, axis=-1, keepdims=True))
                acc = _dot_nt(jnp.exp2(s_diag - m).astype(BF16), vb_scr[hh, :, rows])
                if past:
                    acc = acc + _dot_nt(jnp.exp2(s_past - m).astype(BF16), vb_scr[hh, :, :past])
                maps.append(acc / jnp.where(own, pltpu.roll(acc, DA_HEAD, 1), 1.0))
            heads.append(maps[0] - lam * maps[1])
        o = jnp.where(lane < DA_HEAD, heads[0], heads[1])
        o_ref[rows, :] = _sub_norm_pair(o, lane, gsub_ref[...], lam_init).astype(o_ref.dtype)


def _prompt_attn(q, kt, vt, lam_vecs, gsub2, lam_init):
    b, _, t = kt.shape
    tq = _tile(t, Q_TILE)
    n_pairs = DA_WIDTH // LANES
    assert t <= BIAS_SPLIT * 256
    spec = pl.BlockSpec((t, LANES), lambda bi, hp: (bi, hp))
    spec_t = pl.BlockSpec((None, LANES, t), lambda bi, hp: (bi, hp, 0))
    return pl.pallas_call(
        functools.partial(_prompt_attn_kernel, tq=tq, lam_init=lam_init),
        grid=(b, n_pairs),
        in_specs=[_const_spec(lam_vecs.shape), _const_spec(gsub2.shape), spec, spec_t, spec_t],
        out_specs=spec,
        out_shape=jax.ShapeDtypeStruct((b * t, DA_WIDTH), BF16),
        scratch_shapes=[pltpu.VMEM((2, LANES, t), BF16), pltpu.VMEM((2, LANES, t), BF16)],
        compiler_params=_params(2),
        name="prompt_attn",
    )(lam_vecs, gsub2, q, kt, vt)


def _ffn_unit(refs, n_units, n_ch, tiles_per_seq):
    (x_ref, prev_ref, gn_ref, wup_ref, cw_ref, cb_ref, wdn_ref, y_ref, state_ref, xn_scr, halo_scr, ext_scr) = refs
    hist = FFN_CONV - 1
    top = SUBLANES
    rows, d = x_ref.shape[1], x_ref.shape[2]
    u = pl.program_id(0) * pl.num_programs(1) + pl.program_id(1)
    active = u < n_units
    ua = jnp.minimum(u, n_units - 1)
    rt = ua // n_ch
    ch = ua % n_ch
    first_chunk = active & (ch == 0)

    @pl.when(first_chunk & (rt % tiles_per_seq == 0))
    def _():
        halo_scr[:, 0, top - hist:top, :] = prev_ref[0]

    @pl.when(first_chunk)
    def _():
        xn_scr[...] = _rmsnorm(x_ref[0], gn_ref[...]).astype(BF16)
        y_ref[...] = x_ref[...]

    halves = []

    def conv_half(half):
        j = half * n_ch + ch
        halves.append(_ffn_conv_half(xn_scr[...], 1, rows, wup_ref[j], cw_ref[j], cb_ref[j], halo_scr.at[j], ext_scr))

    def down():
        y_ref[...] += _dot(_ffn_gate(*halves), wdn_ref[ch]).reshape(1, rows, d)

    def finish():
        @pl.when(active & (rt % tiles_per_seq == tiles_per_seq - 1))
        def _():
            for h in range(2):
                state_ref[0, h * n_ch + ch] = halo_scr[h * n_ch + ch, 0, top - hist:top, :]

    return active, (functools.partial(conv_half, 0), functools.partial(conv_half, 1), down), finish


def _sample_attn_kernel(pt_ref, lam_ref, gsub_ref, q_ref, kn_ref, vn_ref, kc_ref, vc_ref, *refs,
                        n_pp, t, past, lam_init, ffn):
    no_stages = (lambda: None,) * 3
    if ffn:
        o_ref = refs[7]
        q_scr, m_scr, l_scr, acc_scr, kbuf, vbuf, sem = refs[10:17]
        ffn_active, ffn_stages, ffn_finish = _ffn_unit(refs[:7] + refs[8:10] + refs[17:], *ffn)
    else:
        o_ref, q_scr, m_scr, l_scr, acc_scr, kbuf, vbuf, sem = refs
    ci = pl.program_id(1)

    step = pl.program_id(0) * pl.num_programs(1) + ci
    n_steps = pl.num_programs(0) * pl.num_programs(1)
    slot = step % 2

    def page_copies(of_step, into_slot):
        copies = []
        for i in range(n_pp):
            pid = pt_ref[of_step * n_pp + i]
            copies.append(pltpu.make_async_copy(kc_ref.at[pid], kbuf.at[into_slot, i], sem.at[into_slot, 0]))
            copies.append(pltpu.make_async_copy(vc_ref.at[pid], vbuf.at[into_slot, i], sem.at[into_slot, 1]))
        return copies

    @pl.when(step == 0)
    def _():
        for cp in page_copies(step, slot):
            cp.start()

    @pl.when(step + 1 < n_steps)
    def _():
        for cp in page_copies(step + 1, 1 - slot):
            cp.start()

    for cp in page_copies(step, slot):
        cp.wait()
    k_pages = [kbuf.at[slot, i] for i in range(n_pp)]
    v_pages = [vbuf.at[slot, i] for i in range(n_pp)]
    rows_h = 2 * t
    n_rows = N_HEADS_DA * rows_h
    tk = n_pp * kbuf.shape[-1]
    head_rows = lambda h: slice(h * rows_h, (h + 1) * rows_h)

    @pl.when(ci == 0)
    def _():
        first_map = lax.broadcasted_iota(jnp.int32, (t, DA_HEAD), 1) < DA_HALF
        for h in range(N_HEADS_DA):
            qh = q_ref[h]
            q_scr[h * rows_h:h * rows_h + t, :] = jnp.where(first_map, qh, 0.0)
            q_scr[h * rows_h + t:(h + 1) * rows_h, :] = jnp.where(first_map, 0.0, qh)
        m_scr[...] = jnp.full(m_scr.shape, NEG, F32)
        l_scr[...] = jnp.zeros(l_scr.shape, F32)
        acc_scr[...] = jnp.zeros(acc_scr.shape, F32)

    r_idx = lax.broadcasted_iota(jnp.int32, (n_rows, 1), 0)
    slope = jnp.exp2((r_idx // rows_h + 1).astype(F32) * (-8.0 / N_HEADS_DA))
    q_pos = r_idx % t

    def scores(keys_t):
        return jnp.concatenate(
            [_dot(q_scr[head_rows(h), :].astype(BF16), keys_t(h)) for h in range(N_HEADS_DA)], axis=0)

    def update(s, weighted_values, between=lambda: None):
        m_old = m_scr[...]
        m_new = jnp.maximum(m_old, jnp.max(s, axis=-1, keepdims=True))
        alpha = jnp.exp(m_old - m_new)
        p = jnp.exp(s - m_new)
        l_scr[...] = alpha * l_scr[...] + jnp.sum(p, axis=-1, keepdims=True)
        pb = p.astype(BF16)
        between()
        pv = jnp.concatenate([weighted_values(h, pb[head_rows(h), :]) for h in range(N_HEADS_DA)], axis=0)
        acc_scr[...] = alpha * acc_scr[...] + pv
        m_scr[...] = m_new

    def cached_pages(stages):
        key_off = lax.broadcasted_iota(jnp.int32, (1, tk), 1) + (ci * tk - past)
        s = scores(lambda h: jnp.concatenate([kp[h] for kp in k_pages], axis=1).astype(BF16))
        s = s - slope * (q_pos - key_off).astype(F32)
        stages[0]()
        update(s, lambda h, ph: _dot_nt(ph, jnp.concatenate([vp[h] for vp in v_pages], axis=1).astype(BF16)),
               between=stages[1])
        stages[2]()

    if ffn:
        pl.when(ffn_active)(lambda: cached_pages(ffn_stages))
        pl.when(jnp.logical_not(ffn_active))(lambda: cached_pages(no_stages))
        ffn_finish()
    else:
        cached_pages(no_stages)

    @pl.when(ci == pl.num_programs(1) - 1)
    def _():
        pad = jnp.zeros((LANES - t, DA_HEAD), F32)
        padded = lambda ref, h: jnp.concatenate([ref[h], pad], axis=0).astype(BF16)
        new_i = lax.broadcasted_iota(jnp.int32, (1, LANES), 1)
        s = jnp.concatenate(
            [_dot_nt(q_scr[head_rows(h), :].astype(BF16), padded(kn_ref, h)) for h in range(N_HEADS_DA)], axis=0)
        s = s - slope * (q_pos - new_i).astype(F32)
        s = jnp.where((new_i < t) & (new_i <= q_pos), s, NEG)
        update(s, lambda h, ph: _dot(ph, padded(vn_ref, h)))

        o = acc_scr[...] / l_scr[...]
        lam = _diff_lambda(lam_ref, lam_init)
        for h in range(N_HEADS_DA):
            d = o[h * rows_h:h * rows_h + t, :] - lam * o[h * rows_h + t:(h + 1) * rows_h, :]
            o_ref[h] = _rmsnorm(d, gsub_ref[...]) * (1.0 - lam_init)


def _pages_per_step(n_pages):
    n_pp = min(PAGES_PER_STEP, n_pages)
    while n_pages % n_pp:
        n_pp -= 1
    return n_pp


def _ffn_units(x3, tt, f):
    n_ch = f // _tile(f, FUSED_FFN_CHUNK)
    return x3.shape[0] * (x3.shape[1] // tt) * n_ch, n_ch, x3.shape[1] // tt


def _sample_attn(page_ids, q4, k_new4, v_new4, cache_kt, cache_vt, lam_vecs, gsub, lam_init, ffn=None):
    n_seq, n_heads, t, dh = q4.shape
    page = cache_kt.shape[-1]
    n_pages = page_ids.shape[0] // n_seq
    n_pp = _pages_per_step(n_pages)
    n_rows = 2 * n_heads * t
    assert (2 * t) % 16 == 0 and t <= LANES

    seq_spec = pl.BlockSpec((None, n_heads, t, dh), lambda s, c, pt: (s, 0, 0, 0))
    const = lambda shape: pl.BlockSpec(shape, lambda s, c, pt: (0,) * len(shape), pipeline_mode=pl.Buffered(1))
    n_chunks = n_pages // n_pp
    in_specs = ([const(lam_vecs.shape), const(gsub.shape), seq_spec, seq_spec, seq_spec]
                + [pl.BlockSpec(memory_space=pl.ANY)] * 2)
    out_specs = [seq_spec]
    out_shape = [jax.ShapeDtypeStruct((n_seq, n_heads, t, dh), F32)]
    scratch = [pltpu.VMEM((n_rows, dh), F32), pltpu.VMEM((n_rows, 1), F32),
               pltpu.VMEM((n_rows, 1), F32), pltpu.VMEM((n_rows, dh), F32),
               pltpu.VMEM((2, n_pp, n_heads, dh, page), F32), pltpu.VMEM((2, n_pp, n_heads, dh, page), F32),
               pltpu.SemaphoreType.DMA((2, 2))]
    operands = [page_ids, lam_vecs, gsub, q4, k_new4, v_new4, cache_kt, cache_vt]
    ffn_static = None
    if ffn is not None:
        x3, prev, gn, wup, cw, cb, wdn, tt = ffn
        b, _, d = x3.shape
        f = wdn.shape[0]
        n_units, n_ch, tiles = _ffn_units(x3, tt, f)
        assert n_units <= n_seq * n_chunks
        fc = f // n_ch
        ffn_static = (n_units, n_ch, tiles)
        chunked = lambda a: jnp.transpose(a.reshape(a.shape[0], 2, n_ch, fc), (1, 2, 0, 3)).reshape(
            2 * n_ch, a.shape[0], fc)
        prev_c = jnp.transpose(prev.reshape(b, FFN_CONV - 1, 2 * n_ch, fc), (0, 2, 1, 3))

        def tile_of(s, c):
            return jnp.minimum(s * n_chunks + c, n_units - 1) // n_ch

        x_spec = pl.BlockSpec((1, tt, d), lambda s, c, pt: (tile_of(s, c) // tiles, tile_of(s, c) % tiles, 0))
        st_spec = pl.BlockSpec((1, 2 * n_ch, FFN_CONV - 1, fc), lambda s, c, pt: (tile_of(s, c) // tiles, 0, 0, 0))
        ffn_in = [x3, prev_c, gn, chunked(wup), chunked(cw), chunked(cb), wdn.reshape(n_ch, fc, d)]
        in_specs += [x_spec, st_spec] + [const(a.shape) for a in ffn_in[2:]]
        out_specs += [x_spec, st_spec]
        out_shape += [jax.ShapeDtypeStruct(x3.shape, F32), jax.ShapeDtypeStruct(prev_c.shape, F32)]
        scratch += [pltpu.VMEM((tt, d), BF16), pltpu.VMEM((2 * n_ch, 1, SUBLANES, fc), F32),
                    pltpu.VMEM((1, SUBLANES + tt, fc), F32)]
        operands += ffn_in
    grid_spec = pltpu.PrefetchScalarGridSpec(
        num_scalar_prefetch=1, grid=(n_seq, n_chunks), in_specs=in_specs, out_specs=out_specs, scratch_shapes=scratch)
    outs = pl.pallas_call(
        functools.partial(_sample_attn_kernel, n_pp=n_pp, t=t, past=n_pages * page, lam_init=lam_init,
                          ffn=ffn_static),
        grid_spec=grid_spec,
        out_shape=out_shape,
        compiler_params=pltpu.CompilerParams(dimension_semantics=("arbitrary",) * 2,
                                             vmem_limit_bytes=FUSED_VMEM_LIMIT if ffn is not None else VMEM_LIMIT),
        name="sample_attn",
    )(*operands)
    if ffn is None:
        return outs[0]
    state = jnp.transpose(outs[2], (0, 2, 1, 3)).reshape(prev.shape)
    return outs[0], outs[1], state


def _lru_kernel(lx_ref, lg_ref, cprev_ref, h0_ref, cw_ref, cb_ref, wa_ref, ba_ref, wx_ref, bx_ref, lam_ref,
                out_ref, hlast_ref, ext_scr, h_scr, a_scr, g_scr, *, g, tt):
    i = pl.program_id(1)
    w = LRU_WIDTH
    hist = LRU_CONV - 1
    top = SUBLANES

    @pl.when(i == 0)
    def _():
        ext_scr[:, top - hist:top, :] = cprev_ref[...]
        h_scr[...] = h0_ref[...]

    ext_scr[:, top:top + tt, :] = lx_ref[...]
    nc = tt // SUBLANES
    ext4 = ext_scr[...].reshape(g, nc + 1, SUBLANES, w)
    sub4 = lax.broadcasted_iota(jnp.int32, (g, nc, SUBLANES, w), 2)
    xc4 = cb_ref[...]
    for j in range(LRU_CONV):
        k = hist - j
        if k:
            rolled = pltpu.roll(ext4, k, 2)
            shifted = jnp.where(sub4 >= k, rolled[:, 1:], rolled[:, :nc])
        else:
            shifted = ext4[:, 1:]
        xc4 = xc4 + shifted * cw_ref[j:j + 1, :]
    ext_scr[:, top - hist:top, :] = ext_scr[:, top + tt - hist:top + tt, :]

    r = g * tt
    xc = xc4.reshape(r, w)
    xb = xc.astype(BF16)
    rg = _sigmoid(_dot(xb, wa_ref[...]) + ba_ref[...])
    ig = _sigmoid(_dot(xb, wx_ref[...]) + bx_ref[...])
    nl = -lam_ref[...]
    softplus = jnp.maximum(nl, 0.0) + jnp.log1p(jnp.exp(-jnp.abs(nl)))
    log_a = -LRU_C * rg * softplus
    a = jnp.exp(log_a)
    one_minus_a2 = -jnp.tanh(log_a) * (a * a + 1.0)
    gx = jnp.where(one_minus_a2 > 0.0, one_minus_a2 * lax.rsqrt(one_minus_a2), 0.0) * (ig * xc)

    nch = r // SUBLANES
    a3 = a.reshape(nch, SUBLANES, w)
    g3 = gx.reshape(nch, SUBLANES, w)
    sub = lax.broadcasted_iota(jnp.int32, (nch, SUBLANES, w), 1)
    for s in (1, 2, 4):
        keep = sub >= s
        a_prev = jnp.where(keep, pltpu.roll(a3, s, 1), 1.0)
        g_prev = jnp.where(keep, pltpu.roll(g3, s, 1), 0.0)
        g3 = a3 * g_prev + g3
        a3 = a3 * a_prev

    nc = tt // SUBLANES
    a_scr[...] = a3.reshape(g, nc, SUBLANES, w)
    g_scr[...] = g3.reshape(g, nc, SUBLANES, w)

    def chunk(c, h):
        hs = a_scr[:, c] * h + g_scr[:, c]
        g_scr[:, c] = hs
        return hs[:, SUBLANES - 1:SUBLANES, :]

    h = h_scr[...]
    if nc <= 2:
        for c in range(nc):
            h = chunk(c, h)
    else:
        h = lax.fori_loop(0, nc, chunk, h)
    h_scr[...] = h

    hs = g_scr[...].reshape(r, w)
    out = hs * jax.nn.gelu(lg_ref[...].reshape(r, w))
    out_ref[...] = out.reshape(g, tt, w).astype(out_ref.dtype)

    @pl.when(i == pl.num_programs(1) - 1)
    def _():
        hlast_ref[...] = h


def _lru(lx3, lg3, conv_prev, h0, cw, cb, wa, ba, wx, bx, lam, g, tt, out_dtype):
    n_seq, t, w = lx3.shape
    blk = pl.BlockSpec((g, tt, w), lambda s, i: (s, i, 0))
    seq = lambda rows: pl.BlockSpec((g, rows, w), lambda s, i: (s, 0, 0))
    return pl.pallas_call(
        functools.partial(_lru_kernel, g=g, tt=tt),
        grid=(n_seq // g, t // tt),
        in_specs=[blk, blk, seq(LRU_CONV - 1), seq(1)] + [_const_spec(a.shape) for a in (cw, cb, wa, ba, wx, bx, lam)],
        out_specs=[blk, seq(1)],
        out_shape=[jax.ShapeDtypeStruct((n_seq, t, w), out_dtype), jax.ShapeDtypeStruct((n_seq, 1, w), F32)],
        scratch_shapes=[pltpu.VMEM((g, SUBLANES + tt, w), F32), pltpu.VMEM((g, 1, w), F32),
                        pltpu.VMEM((g, tt // SUBLANES, SUBLANES, w), F32),
                        pltpu.VMEM((g, tt // SUBLANES, SUBLANES, w), F32)],
        compiler_params=_params(2),
        name="lru",
    )(lx3, lg3, conv_prev, h0, cw, cb, wa, ba, wx, bx, lam)


def _out_cross_kernel(x_ref, da_ref, lru_ref, mk_ref, mv_ref, wo_da_ref, wo_lru_ref, gn_ref, wcq_ref, gcq_ref,
                      wco_ref, out_ref, q_scr, o_scr, *, g, tt):
    r = g * tt
    d = x_ref.shape[-1]
    h = (x_ref[...].reshape(r, d)
         + _dot(da_ref[...].reshape(r, DA_WIDTH).astype(BF16), wo_da_ref[...])
         + _dot(lru_ref[...].reshape(r, LRU_WIDTH).astype(BF16), wo_lru_ref[...]))
    hn = _rmsnorm(h, gn_ref[...]).astype(BF16)
    qc = _dot(hn, wcq_ref[...])
    for hd in range(N_HEADS_X):
        sl = slice(hd * X_HEAD, (hd + 1) * X_HEAD)
        q_scr[:, sl] = _rmsnorm(qc[:, sl], gcq_ref[...])

    if mk_ref.shape[-1] == X_WIDTH:
        for hd in range(N_HEADS_X):
            sl = slice(hd * X_HEAD, (hd + 1) * X_HEAD)
            s = _dot_nt(q_scr[:, sl].astype(BF16), mk_ref[0, :, sl].astype(BF16))
            e = jnp.exp(s - jnp.max(s, axis=-1, keepdims=True))
            o = _dot(e.astype(BF16), mv_ref[0, :, sl].astype(BF16))
            o_scr[:, sl] = o / jnp.sum(e, axis=-1, keepdims=True)
    else:
        n_rows = N_HEADS_X * tt
        n_cols = mk_ref.shape[1]
        col_head = lax.broadcasted_iota(jnp.int32, (n_rows, n_cols), 1) % N_HEADS_X
        own_head = col_head == lax.broadcasted_iota(jnp.int32, (n_rows, n_cols), 0) // tt

        def one_seq(si, carry):
            r0 = pl.multiple_of(si * tt, tt)
            q = q_scr[pl.ds(r0, tt), :]
            qs = jnp.concatenate([q[:, hd * X_HEAD:(hd + 1) * X_HEAD] for hd in range(N_HEADS_X)], axis=0)
            s = jnp.where(own_head, _dot_nt(qs.astype(BF16), mk_ref[si].astype(BF16)), NEG)
            e = jnp.exp(s - jnp.max(s, axis=-1, keepdims=True))
            o = _dot(e.astype(BF16), mv_ref[si].astype(BF16)) / jnp.sum(e, axis=-1, keepdims=True)
            for hd in range(N_HEADS_X):
                o_scr[pl.ds(r0, tt), hd * X_HEAD:(hd + 1) * X_HEAD] = o[hd * tt:(hd + 1) * tt, :]
            return carry

        lax.fori_loop(0, g, one_seq, 0)

    out = h + _dot(o_scr[...].astype(BF16), wco_ref[...])
    out_ref[...] = out.reshape(g, tt, d)


def _out_cross(x3, da3, lru3, mk3, mv3, mem_first, wo_da, wo_lru, gn, wcq, gcq, wco, g, tt):
    n_seq, t, d = x3.shape
    assert mk3.shape[-1] == X_HEAD or (mk3.shape[-1] == X_WIDTH and g == 1)
    mem_blk0 = mem_first // g
    blk = lambda w: pl.BlockSpec((g, tt, w), lambda s, i: (s, i, 0))
    mem = pl.BlockSpec((g,) + mk3.shape[1:], lambda s, i: (s + mem_blk0, 0, 0))
    return pl.pallas_call(
        functools.partial(_out_cross_kernel, g=g, tt=tt),
        grid=(n_seq // g, t // tt),
        in_specs=[blk(d), blk(DA_WIDTH), blk(LRU_WIDTH), mem, mem]
                 + [_const_spec(a.shape) for a in (wo_da, wo_lru, gn, wcq, gcq, wco)],
        out_specs=blk(d),
        out_shape=jax.ShapeDtypeStruct((n_seq, t, d), F32),
        scratch_shapes=[pltpu.VMEM((g * tt, X_WIDTH), F32), pltpu.VMEM((g * tt, X_WIDTH), F32)],
        compiler_params=_params(2),
        name="out_cross",
    )(x3, da3, lru3, mk3, mv3, wo_da, wo_lru, gn, wcq, gcq, wco)


def _ffn_conv_half(xn, g, tt, up_w, conv_w, conv_b, h_ref, ext_scr):
    hist = FFN_CONV - 1
    top = SUBLANES
    fc = ext_scr.shape[-1]
    ext_scr[:, top:top + tt, :] = _dot(xn, up_w).reshape(g, tt, fc)
    ext_scr[:, top - hist:top, :] = h_ref[:, top - hist:top, :]
    c3 = conv_b
    for j in range(FFN_CONV):
        c3 = c3 + ext_scr[:, top - hist + j:top - hist + j + tt, :] * conv_w[j:j + 1, :]
    h_ref[:, top - hist:top, :] = ext_scr[:, top + tt - hist:top + tt, :]
    return c3.reshape(g * tt, fc)


def _ffn_gate(gate, value):
    return (gate * _sigmoid(gate) * value).astype(BF16)


def _conv_ffn_kernel(x_ref, prev_ref, gn_ref, wup_ref, cw_ref, cb_ref, wdn_ref, out_ref, state_ref,
                     halo_scr, ext_scr, *, g, tt, fc):
    i = pl.program_id(1)
    r = g * tt
    d = x_ref.shape[-1]
    f = wdn_ref.shape[0]
    hist = FFN_CONV - 1
    top = SUBLANES

    @pl.when(i == 0)
    def _():
        halo_scr[:, top - hist:top, :] = prev_ref[...]

    xn = _rmsnorm(x_ref[...].reshape(r, d), gn_ref[...]).astype(BF16)
    out_ref[...] = x_ref[...]

    for ch in range(f // fc):
        halves = []
        for half in range(2):
            cols = slice(half * f + ch * fc, half * f + (ch + 1) * fc)
            halves.append(_ffn_conv_half(xn, g, tt, wup_ref[:, cols], cw_ref[:, cols], cb_ref[:, cols],
                                         halo_scr.at[:, :, cols], ext_scr))
        out_ref[...] += _dot(_ffn_gate(*halves), wdn_ref[ch * fc:(ch + 1) * fc, :]).reshape(g, tt, d)

    @pl.when(i == pl.num_programs(1) - 1)
    def _():
        state_ref[...] = halo_scr[:, top - hist:top, :]


def _conv_ffn(x3, prev, gn, wup, cw, cb, wdn, g, tt):
    n_seq, t, d = x3.shape
    f2 = wup.shape[1]
    fc = _tile(f2 // 2, FFN_CHUNK)
    blk = pl.BlockSpec((g, tt, d), lambda s, i: (s, i, 0))
    st = pl.BlockSpec((g, FFN_CONV - 1, f2), lambda s, i: (s, 0, 0))
    return pl.pallas_call(
        functools.partial(_conv_ffn_kernel, g=g, tt=tt, fc=fc),
        grid=(n_seq // g, t // tt),
        in_specs=[blk, st] + [_const_spec(a.shape) for a in (gn, wup, cw, cb, wdn)],
        out_specs=[blk, st],
        out_shape=[jax.ShapeDtypeStruct((n_seq, t, d), F32), jax.ShapeDtypeStruct((n_seq, FFN_CONV - 1, f2), F32)],
        scratch_shapes=[pltpu.VMEM((g, SUBLANES, f2), F32), pltpu.VMEM((g, SUBLANES + tt, fc), F32)],
        compiler_params=_params(2),
        name="conv_ffn",
    )(x3, prev, gn, wup, cw, cb, wdn)


def _block_diag(blocks):
    n, a, b = blocks.shape
    eye = jnp.eye(n, dtype=blocks.dtype)
    return (eye[:, None, :, None] * blocks[:, :, None, :]).reshape(n * a, n * b)


def _layer_weights(l, norm_mix, w_in, g_q, g_k, lam_q1, lam_k1, lam_q2, lam_k2, g_sub, lru_conv_w, lru_conv_b,
                   lru_wa, lru_ba, lru_wx, lru_bx, lru_lambda, w_o, norm_cross, norm_mem, w_cq, w_ck, w_cv,
                   g_cq, g_ck, w_co, norm_ffn, w_up, ffn_conv_w, ffn_conv_b, w_down):
    row = lambda a: a[l].reshape(1, -1)
    n_maps = DA_WIDTH // DA_HALF
    return dict(
        norm_mix=row(norm_mix), w_in=w_in[l].astype(BF16),
        pmat=jnp.kron(jnp.eye(n_maps, dtype=F32), jnp.full((DA_HALF, DA_HALF), 1.0 / DA_HALF, F32)).astype(BF16),
        gq=jnp.tile(g_q[l] * (DA_HALF ** -0.5), n_maps).reshape(1, -1),
        gq_base2=jnp.tile(g_q[l] * (DA_HALF ** -0.5 * LOG2E), n_maps).reshape(1, -1),
        gk=jnp.tile(g_k[l], n_maps).reshape(1, -1),
        lam=jnp.stack([lam_q1[l], lam_k1[l], lam_q2[l], lam_k2[l]]),
        gsub2=jnp.tile(g_sub[l], LANES // DA_HEAD).reshape(1, -1),
        gsub=row(g_sub),
        cw=lru_conv_w[l], cb=row(lru_conv_b),
        wa=_block_diag(lru_wa[l]).astype(BF16), ba=row(lru_ba),
        wx=_block_diag(lru_wx[l]).astype(BF16), bx=row(lru_bx), lru_lambda=row(lru_lambda),
        wo_da=w_o[l, :DA_WIDTH].astype(BF16), wo_lru=w_o[l, DA_WIDTH:].astype(BF16),
        norm_cross=row(norm_cross), norm_mem=row(norm_mem),
        wcq=w_cq[l].astype(BF16), wck=w_ck[l].astype(BF16), wcv=w_cv[l].astype(BF16),
        gcq=(g_cq[l] * (X_HEAD ** -0.5)).reshape(1, -1), gck=row(g_ck), wco=w_co[l].astype(BF16),
        norm_ffn=row(norm_ffn), wup=w_up[l].astype(BF16), fcw=ffn_conv_w[l], fcb=row(ffn_conv_b),
        wdn=w_down[l].astype(BF16),
    )


def _mix(x3, da3, lx, lg, conv_prev, h0, mk3, mv3, mem_first, p, g, tt, mid_dtype):
    n_seq, t, d = x3.shape
    assert t >= LRU_CONV - 1 and t >= FFN_CONV - 1
    lru3, h_last = _lru(lx.reshape(n_seq, t, LRU_WIDTH), lg.reshape(n_seq, t, LRU_WIDTH), conv_prev, h0,
                        p['cw'], p['cb'], p['wa'], p['ba'], p['wx'], p['bx'], p['lru_lambda'], g, tt, mid_dtype)
    g_x = g
    while g_x > 1 and 4 * g_x * mk3.shape[1] * mk3.shape[2] * 4 > MEM_BLOCK_BYTES:
        g_x //= 2
    h3 = _out_cross(x3, da3, lru3, mk3, mv3, mem_first, p['wo_da'], p['wo_lru'], p['norm_cross'], p['wcq'], p['gcq'],
                    p['wco'], g_x, tt)
    conv_state = lx.reshape(n_seq, t, LRU_WIDTH)[:, t - (LRU_CONV - 1):]
    return h3, conv_state, h_last.reshape(n_seq, LRU_WIDTH)


def kernel(x_prompt, x_sample, mem_prompt, cache_k, cache_v, page_table, cache_mem_k, cache_mem_v, state_lru_conv, state_lru_h, state_ffn_conv, norm_mix, w_in, g_q, g_k, lam_q1, lam_k1, lam_q2, lam_k2, g_sub, lru_conv_w, lru_conv_b, lru_wa, lru_ba, lru_wx, lru_bx, lru_lambda, w_o, norm_cross, norm_mem, w_cq, w_ck, w_cv, g_cq, g_ck, w_co, norm_ffn, w_up, ffn_conv_w, ffn_conv_b, w_down):
    depth = w_in.shape[0]
    b, t, d = x_prompt.shape
    nb, nt, _ = x_sample.shape
    n_mem = mem_prompt.shape[1]
    n_phys, page = cache_k.shape[1], cache_k.shape[2]
    f2 = w_up.shape[2]
    to_pages = lambda c: jnp.transpose(c, (0, 1, 3, 4, 2)).reshape(depth * n_phys, N_HEADS_DA, DA_HEAD, page)
    cache_kt, cache_vt = to_pages(cache_k), to_pages(cache_v)
    to_heads = lambda a, n, m: jnp.transpose(a.reshape(n, m, N_HEADS_DA, DA_HEAD), (0, 2, 1, 3))
    cmk = cache_mem_k.reshape(depth * nb, n_mem * N_HEADS_X, X_HEAD)
    cmv = cache_mem_v.reshape(depth * nb, n_mem * N_HEADS_X, X_HEAD)

    tt_p = _tile(t, ROW_TILE)
    g_s = SAMPLE_SEQS if nb % SAMPLE_SEQS == 0 else 1

    yp, ys = x_prompt, x_sample
    outs_p, outs_s = [], []
    for l in range(depth):
        p = _layer_weights(l, norm_mix, w_in, g_q, g_k, lam_q1, lam_k1, lam_q2, lam_k2, g_sub, lru_conv_w,
                           lru_conv_b, lru_wa, lru_ba, lru_wx, lru_bx, lru_lambda, w_o, norm_cross, norm_mem,
                           w_cq, w_ck, w_cv, g_cq, g_ck, w_co, norm_ffn, w_up, ffn_conv_w, ffn_conv_b, w_down)
        lam_init = 0.8 - 0.6 * math.exp(-0.3 * l)

        q, kt, vt, lx, lg = _in_proj(yp.reshape(b * t, d), p['norm_mix'], p['w_in'], p['pmat'], p['gq_base2'], p['gk'],
                                     BF16, seq_len=t)
        da = _prompt_attn(q, kt, vt, p['lam'], p['gsub2'], lam_init)
        from_t = lambda a: jnp.transpose(a.reshape(b, N_HEADS_DA, DA_HEAD, t), (0, 3, 1, 2))
        mk, mv = _mem_kv(mem_prompt.reshape(b * n_mem, d), p['norm_mem'], p['wck'], p['wcv'], p['gck'])
        hp, conv_state_p, h_last_p = _mix(
            yp, da.reshape(b, t, DA_WIDTH), lx, lg,
            jnp.zeros((b, LRU_CONV - 1, LRU_WIDTH), F32), jnp.zeros((b, 1, LRU_WIDTH), F32),
            mk.reshape(b, n_mem, X_WIDTH), mv.reshape(b, n_mem, X_WIDTH), 0, p, 1, tt_p, BF16)
        ffn_p = (hp, jnp.zeros((b, FFN_CONV - 1, f2), F32), p['norm_ffn'], p['wup'], p['fcw'], p['fcb'], p['wdn'])

        q, k, v, lx, lg = _in_proj(ys.reshape(nb * nt, d), p['norm_mix'], p['w_in'], p['pmat'], p['gq'], p['gk'], F32)
        page_ids = page_table.reshape(-1).astype(jnp.int32) + l * n_phys
        attn_args = (page_ids, to_heads(q, nb, nt), to_heads(k, nb, nt), to_heads(v, nb, nt),
                     cache_kt, cache_vt, p['lam'], p['gsub'], lam_init)
        n_pages = page_table.shape[1]
        if _ffn_units(hp, tt_p, f2 // 2)[0] <= nb * (n_pages // _pages_per_step(n_pages)):
            da, yp, ffn_state_p = _sample_attn(*attn_args, ffn=ffn_p + (tt_p,))
        else:
            da = _sample_attn(*attn_args)
            yp, ffn_state_p = _conv_ffn(*ffn_p, 1, tt_p)
        outs_p.append((from_t(kt), from_t(vt),
                       mk.reshape(b, n_mem, N_HEADS_X, X_HEAD), mv.reshape(b, n_mem, N_HEADS_X, X_HEAD),
                       conv_state_p, h_last_p, ffn_state_p))
        da = jnp.transpose(da, (0, 2, 1, 3))
        hs, conv_state, h_last = _mix(
            ys, da.reshape(nb, nt, DA_WIDTH), lx, lg,
            state_lru_conv[l], state_lru_h[l].reshape(nb, 1, LRU_WIDTH),
            cmk, cmv, l * nb, p, g_s, nt, F32)
        ys, ffn_state = _conv_ffn(hs, state_ffn_conv[l], p['norm_ffn'], p['wup'], p['fcw'], p['fcb'], p['wdn'], g_s, nt)
        outs_s.append((k.reshape(nb, nt, N_HEADS_DA, DA_HEAD), v.reshape(nb, nt, N_HEADS_DA, DA_HEAD),
                       conv_state, h_last, ffn_state))

    stack = lambda outs, j: jnp.stack([o[j] for o in outs])
    return (yp, ys, stack(outs_p, 0), stack(outs_p, 1), stack(outs_s, 0), stack(outs_s, 1),
            stack(outs_p, 2), stack(outs_p, 3), stack(outs_p, 4), stack(outs_s, 2),
            stack(outs_p, 5), stack(outs_s, 3), stack(outs_p, 6), stack(outs_s, 4))
```

```python
import functools
import math

import jax
import jax.numpy as jnp
from jax import lax
from jax.experimental import pallas as pl
from jax.experimental.pallas import tpu as pltpu

F32 = jnp.float32
BF16 = jnp.bfloat16

EPS = 1e-6
N_HEADS_DA = 8
DA_HEAD = 64
DA_HALF = DA_HEAD // 2
DA_WIDTH = N_HEADS_DA * DA_HEAD
LRU_WIDTH = 512
LRU_CONV = 4
LRU_C = 8.0
N_HEADS_X = 4
X_HEAD = 128
X_WIDTH = N_HEADS_X * X_HEAD
FFN_CONV = 3

LANES = 128
SUBLANES = 8
NEG = -1e30
VMEM_LIMIT = 48 * 1024 * 1024

ROW_TILE = 512
Q_TILE = 256
PAGES_PER_STEP = 16
FFN_CHUNK = 256
FUSED_VMEM_LIMIT = 56 * 1024 * 1024
SAMPLE_SEQS = 32
MEM_BLOCK_BYTES = 16 * 1024 * 1024


def _tile(n, pref):
    t = min(n, pref)
    while n % t:
        t -= SUBLANES
    return t


def _const_spec(shape):
    zeros = (0,) * len(shape)
    return pl.BlockSpec(shape, lambda *_: zeros, pipeline_mode=pl.Buffered(1))


def _params(n_axes):
    return pltpu.CompilerParams(dimension_semantics=("arbitrary",) * n_axes, vmem_limit_bytes=VMEM_LIMIT)


def _rmsnorm(x, g):
    return x * lax.rsqrt(jnp.mean(x * x, axis=-1, keepdims=True) + EPS) * g


def _sigmoid(x):
    return 0.5 * jnp.tanh(0.5 * x) + 0.5


def _dot(a, b):
    return jnp.dot(a, b, preferred_element_type=F32)


def _dot_nt(a, b):
    return lax.dot_general(a, b, (((1,), (1,)), ((), ())), preferred_element_type=F32)


def _diff_lambda(lam_ref, lam_init):
    v = lam_ref[...]
    s1 = jnp.sum(v[0:1] * v[1:2], axis=-1, keepdims=True)
    s2 = jnp.sum(v[2:3] * v[3:4], axis=-1, keepdims=True)
    return jnp.exp(s1) - jnp.exp(s2) + lam_init


def _in_proj_kernel(x_ref, gn_ref, w_ref, p_ref, gq_ref, gk_ref, q_ref, k_ref, v_ref, lx_ref, lg_ref, *scratch):
    xn = _rmsnorm(x_ref[...], gn_ref[...]).astype(BF16)

    def proj(j):
        return _dot(xn, w_ref[:, j * DA_WIDTH:(j + 1) * DA_WIDTH])

    def map_norm(z, g):
        sq = z * z
        hi = sq.astype(BF16)
        lo = (sq - hi.astype(F32)).astype(BF16)
        ms = _dot(hi, p_ref[...]) + _dot(lo, p_ref[...])
        return z * lax.rsqrt(ms + EPS) * g

    q_ref[...] = map_norm(proj(0), gq_ref[...]).astype(q_ref.dtype)
    k = map_norm(proj(1), gk_ref[...])
    if scratch:
        v_scr, = scratch
        v_scr[...] = proj(2)
        k_ref[...] = k.T
        v_ref[...] = v_scr[...].T
    else:
        k_ref[...] = k
        v_ref[...] = proj(2)
    lx_ref[...] = proj(3)
    lg_ref[...] = proj(4)


def _in_proj(x2d, gn, w_in, pmat, gq, gk, q_dtype, seq_len=None):
    n, d = x2d.shape
    r = _tile(seq_len or n, ROW_TILE)
    row = lambda w: pl.BlockSpec((r, w), lambda i: (i, 0))
    out_sd = lambda dt: jax.ShapeDtypeStruct((n, DA_WIDTH), dt)
    kv_spec, kv_sd = row(DA_WIDTH), out_sd(F32)
    if seq_len:
        tiles = seq_len // r
        kv_spec = pl.BlockSpec((None, DA_WIDTH, r), lambda i: (i // tiles, 0, i % tiles))
        kv_sd = jax.ShapeDtypeStruct((n // seq_len, DA_WIDTH, seq_len), F32)
    return pl.pallas_call(
        _in_proj_kernel,
        grid=(n // r,),
        in_specs=[row(d), _const_spec(gn.shape), _const_spec(w_in.shape), _const_spec(pmat.shape),
                  _const_spec(gq.shape), _const_spec(gk.shape)],
        out_specs=[row(DA_WIDTH), kv_spec, kv_spec, row(DA_WIDTH), row(DA_WIDTH)],
        out_shape=[out_sd(q_dtype), kv_sd, kv_sd, out_sd(F32), out_sd(F32)],
        scratch_shapes=[pltpu.VMEM((r, DA_WIDTH), F32)] if seq_len else [],
        compiler_params=_params(1),
        name="in_proj",
    )(x2d, gn, w_in, pmat, gq, gk)


def _mem_kv_kernel(m_ref, gn_ref, wk_ref, wv_ref, gck_ref, k_ref, v_ref):
    mn = _rmsnorm(m_ref[...], gn_ref[...]).astype(BF16)
    k = _dot(mn, wk_ref[...])
    for h in range(N_HEADS_X):
        sl = slice(h * X_HEAD, (h + 1) * X_HEAD)
        k_ref[:, sl] = _rmsnorm(k[:, sl], gck_ref[...])
    v_ref[...] = _dot(mn, wv_ref[...])


def _mem_kv(mem2d, gn, w_ck, w_cv, g_ck):
    n, d = mem2d.shape
    r = _tile(n, ROW_TILE)
    row = lambda w: pl.BlockSpec((r, w), lambda i: (i, 0))
    return pl.pallas_call(
        _mem_kv_kernel,
        grid=(n // r,),
        in_specs=[row(d), _const_spec(gn.shape), _const_spec(w_ck.shape), _const_spec(w_cv.shape),
                  _const_spec(g_ck.shape)],
        out_specs=[row(X_WIDTH)] * 2,
        out_shape=[jax.ShapeDtypeStruct((n, X_WIDTH), F32)] * 2,
        compiler_params=_params(1),
        name="mem_kv",
    )(mem2d, gn, w_ck, w_cv, g_ck)


def _head_slope(head, shape):
    e = (head + 1).astype(F32) * (-8.0 / N_HEADS_DA)
    return jnp.exp2(jnp.full(shape, e, F32))


def _sub_norm_pair(o, lane, gsub, lam_init):
    sq = o * o
    first = lane < DA_HEAD
    ms0 = jnp.sum(jnp.where(first, sq, 0.0), axis=-1, keepdims=True)
    ms1 = jnp.sum(jnp.where(first, 0.0, sq), axis=-1, keepdims=True)
    ms = jnp.where(first, ms0, ms1) * (1.0 / DA_HEAD)
    return (o * lax.rsqrt(ms + EPS) * gsub) * (1.0 - lam_init)


BIAS_SPLIT = 256


N_BIAS_TERMS = 3
LOG2E = math.log2(math.e)


def _prompt_attn_kernel(lam_ref, gsub_ref, q_ref, kt_ref, vt_ref, o_ref, kb_scr, vb_scr, *, tq, lam_init):
    hp = pl.program_id(1)
    t = kt_ref.shape[1]
    other = lambda hh: (1 - hh) * DA_HEAD

    slab_row = lax.broadcasted_iota(jnp.int32, (LANES, t), 0)
    pos = lax.broadcasted_iota(jnp.int32, (LANES, t), 1)
    pos_hi = (pos // BIAS_SPLIT * BIAS_SPLIT).astype(F32)
    pos_lo = (pos % BIAS_SPLIT).astype(F32)
    for hh in range(2):
        own = (slab_row >= hh * DA_HEAD) & (slab_row < (hh + 1) * DA_HEAD)
        rel = slab_row - other(hh)
        in_bias = (rel >= 0) & (rel < 2 * N_BIAS_TERMS)
        bias = jnp.where(in_bias, jnp.where(rel % 2 == 0, pos_hi, pos_lo), 0.0)
        kb_scr[hh] = jnp.where(own, kt_ref[...], bias).astype(BF16)
        vb_scr[hh] = jnp.where(own, vt_ref[...], 1.0).astype(BF16)

    lane = lax.broadcasted_iota(jnp.int32, (tq, LANES), 1)
    visible = (lax.broadcasted_iota(jnp.int32, (tq, tq), 1) <= lax.broadcasted_iota(jnp.int32, (tq, tq), 0))
    lam = _diff_lambda(lam_ref, lam_init)

    def q_slab(q, hh, c):
        lo = hh * DA_HEAD + c * DA_HALF
        rel = lane - other(hh)
        rest = _head_slope(hp * 2 + hh, (1, LANES)) * LOG2E
        factor = jnp.zeros((tq, LANES), F32)
        for piece in range(N_BIAS_TERMS):
            part = rest.astype(BF16).astype(F32)
            factor = jnp.where((rel >= 2 * piece) & (rel < 2 * piece + 2), part, factor)
            rest = rest - part
        return jnp.where((lane >= lo) & (lane < lo + DA_HALF), q, factor).astype(BF16)

    for qi in range(t // tq):
        rows = slice(qi * tq, (qi + 1) * tq)
        past = qi * tq
        q = q_ref[rows, :].astype(F32)
        heads = []
        for hh in range(2):
            own = (lane >= hh * DA_HEAD) & (lane < (hh + 1) * DA_HEAD)
            maps = []
            for c in range(2):
                qm = q_slab(q, hh, c)
                s_diag = jnp.where(visible, _dot(qm, kb_scr[hh, :, rows]), NEG)
                m = jnp.max(s_diag, axis=-1, keepdims=True)
                if past:
                    s_past = _dot(qm, kb_scr[hh, :, :past])
                    m = jnp.maximum(m, jnp.max(s_past, axis=-1, keepdims=True))
                acc = _dot_nt(jnp.exp2(s_diag - m).astype(BF16), vb_scr[hh, :, rows])
                if past:
                    acc = acc + _dot_nt(jnp.exp2(s_past - m).astype(BF16), vb_scr[hh, :, :past])
                maps.append(acc / jnp.where(own, pltpu.roll(acc, DA_HEAD, 1), 1.0))
            heads.append(maps[0] - lam * maps[1])
        o = jnp.where(lane < DA_HEAD, heads[0], heads[1])
        o_ref[rows, :] = _sub_norm_pair(o, lane, gsub_ref[...], lam_init).astype(o_ref.dtype)


def _prompt_attn(q, kt, vt, lam_vecs, gsub2, lam_init):
    b, _, t = kt.shape
    tq = _tile(t, Q_TILE)
    n_pairs = DA_WIDTH // LANES
    assert t <= BIAS_SPLIT * 256
    spec = pl.BlockSpec((t, LANES), lambda bi, hp: (bi, hp))
    spec_t = pl.BlockSpec((None, LANES, t), lambda bi, hp: (bi, hp, 0))
    return pl.pallas_call(
        functools.partial(_prompt_attn_kernel, tq=tq, lam_init=lam_init),
        grid=(b, n_pairs),
        in_specs=[_const_spec(lam_vecs.shape), _const_spec(gsub2.shape), spec, spec_t, spec_t],
        out_specs=spec,
        out_shape=jax.ShapeDtypeStruct((b * t, DA_WIDTH), BF16),
        scratch_shapes=[pltpu.VMEM((2, LANES, t), BF16), pltpu.VMEM((2, LANES, t), BF16)],
        compiler_params=_params(2),
        name="prompt_attn",
    )(lam_vecs, gsub2, q, kt, vt)


def _ffn_unit(refs, n_units, n_ch, tiles_per_seq):
    (x_ref, prev_ref, gn_ref, wup_ref, cw_ref, cb_ref, wdn_ref, y_ref, state_ref, xn_scr, halo_scr, ext_scr) = refs
    hist = FFN_CONV - 1
    top = SUBLANES
    rows, d = x_ref.shape[1], x_ref.shape[2]
    u = pl.program_id(0) * pl.num_programs(1) + pl.program_id(1)
    active = u < n_units
    ua = jnp.minimum(u, n_units - 1)
    rt = ua // n_ch
    ch = ua % n_ch
    first_chunk = active & (ch == 0)

    @pl.when(first_chunk & (rt % tiles_per_seq == 0))
    def _():
        halo_scr[:, 0, top - hist:top, :] = prev_ref[0]

    @pl.when(first_chunk)
    def _():
        xn_scr[...] = _rmsnorm(x_ref[0], gn_ref[...]).astype(BF16)
        y_ref[...] = x_ref[...]

    halves = []

    def conv_half(half):
        j = half * n_ch + ch
        halves.append(_ffn_conv_half(xn_scr[...], 1, rows, wup_ref[j], cw_ref[j], cb_ref[j], halo_scr.at[j], ext_scr))

    def down():
        y_ref[...] += _dot(_ffn_gate(*halves), wdn_ref[ch]).reshape(1, rows, d)

    def finish():
        @pl.when(active & (rt % tiles_per_seq == tiles_per_seq - 1))
        def _():
            for h in range(2):
                state_ref[0, h * n_ch + ch] = halo_scr[h * n_ch + ch, 0, top - hist:top, :]

    return active, (functools.partial(conv_half, 0), functools.partial(conv_half, 1), down), finish


def _sample_attn_kernel(pt_ref, lam_ref, gsub_ref, q_ref, kn_ref, vn_ref, kc_ref, vc_ref, *refs,
                        n_pp, t, past, lam_init, ffn):
    no_stages = (lambda: None,) * 3
    if ffn:
        o_ref = refs[7]
        q_scr, m_scr, l_scr, acc_scr, kbuf, vbuf, sem = refs[10:17]
        ffn_active, ffn_stages, ffn_finish = _ffn_unit(refs[:7] + refs[8:10] + refs[17:], *ffn)
    else:
        o_ref, q_scr, m_scr, l_scr, acc_scr, kbuf, vbuf, sem = refs
    ci = pl.program_id(1)

    step = pl.program_id(0) * pl.num_programs(1) + ci
    n_steps = pl.num_programs(0) * pl.num_programs(1)
    slot = step % 2

    def page_copies(of_step, into_slot):
        copies = []
        for i in range(n_pp):
            pid = pt_ref[of_step * n_pp + i]
            copies.append(pltpu.make_async_copy(kc_ref.at[pid], kbuf.at[into_slot, i], sem.at[into_slot, 0]))
            copies.append(pltpu.make_async_copy(vc_ref.at[pid], vbuf.at[into_slot, i], sem.at[into_slot, 1]))
        return copies

    @pl.when(step == 0)
    def _():
        for cp in page_copies(step, slot):
            cp.start()

    @pl.when(step + 1 < n_steps)
    def _():
        for cp in page_copies(step + 1, 1 - slot):
            cp.start()

    for cp in page_copies(step, slot):
        cp.wait()
    k_pages = [kbuf.at[slot, i] for i in range(n_pp)]
    v_pages = [vbuf.at[slot, i] for i in range(n_pp)]
    rows_h = 2 * t
    n_rows = N_HEADS_DA * rows_h
    tk = n_pp * kbuf.shape[-1]
    head_rows = lambda h: slice(h * rows_h, (h + 1) * rows_h)

    @pl.when(ci == 0)
    def _():
        first_map = lax.broadcasted_iota(jnp.int32, (t, DA_HEAD), 1) < DA_HALF
        for h in range(N_HEADS_DA):
            qh = q_ref[h]
            q_scr[h * rows_h:h * rows_h + t, :] = jnp.where(first_map, qh, 0.0)
            q_scr[h * rows_h + t:(h + 1) * rows_h, :] = jnp.where(first_map, 0.0, qh)
        m_scr[...] = jnp.full(m_scr.shape, NEG, F32)
        l_scr[...] = jnp.zeros(l_scr.shape, F32)
        acc_scr[...] = jnp.zeros(acc_scr.shape, F32)

    r_idx = lax.broadcasted_iota(jnp.int32, (n_rows, 1), 0)
    slope = jnp.exp2((r_idx // rows_h + 1).astype(F32) * (-8.0 / N_HEADS_DA))
    q_pos = r_idx % t

    def scores(keys_t):
        return jnp.concatenate(
            [_dot(q_scr[head_rows(h), :].astype(BF16), keys_t(h)) for h in range(N_HEADS_DA)], axis=0)

    def update(s, weighted_values, between=lambda: None):
        m_old = m_scr[...]
        m_new = jnp.maximum(m_old, jnp.max(s, axis=-1, keepdims=True))
        alpha = jnp.exp(m_old - m_new)
        p = jnp.exp(s - m_new)
        l_scr[...] = alpha * l_scr[...] + jnp.sum(p, axis=-1, keepdims=True)
        pb = p.astype(BF16)
        between()
        pv = jnp.concatenate([weighted_values(h, pb[head_rows(h), :]) for h in range(N_HEADS_DA)], axis=0)
        acc_scr[...] = alpha * acc_scr[...] + pv
        m_scr[...] = m_new

    def cached_pages(stages):
        key_off = lax.broadcasted_iota(jnp.int32, (1, tk), 1) + (ci * tk - past)
        s = scores(lambda h: jnp.concatenate([kp[h] for kp in k_pages], axis=1).astype(BF16))
        s = s - slope * (q_pos - key_off).astype(F32)
        stages[0]()
        update(s, lambda h, ph: _dot_nt(ph, jnp.concatenate([vp[h] for vp in v_pages], axis=1).astype(BF16)),
               between=stages[1])
        stages[2]()

    if ffn:
        pl.when(ffn_active)(lambda: cached_pages(ffn_stages))
        pl.when(jnp.logical_not(ffn_active))(lambda: cached_pages(no_stages))
        ffn_finish()
    else:
        cached_pages(no_stages)

    @pl.when(ci == pl.num_programs(1) - 1)
    def _():
        pad = jnp.zeros((LANES - t, DA_HEAD), F32)
        padded = lambda ref, h: jnp.concatenate([ref[h], pad], axis=0).astype(BF16)
        new_i = lax.broadcasted_iota(jnp.int32, (1, LANES), 1)
        s = jnp.concatenate(
            [_dot_nt(q_scr[head_rows(h), :].astype(BF16), padded(kn_ref, h)) for h in range(N_HEADS_DA)], axis=0)
        s = s - slope * (q_pos - new_i).astype(F32)
        s = jnp.where((new_i < t) & (new_i <= q_pos), s, NEG)
        update(s, lambda h, ph: _dot(ph, padded(vn_ref, h)))

        o = acc_scr[...] / l_scr[...]
        lam = _diff_lambda(lam_ref, lam_init)
        for h in range(N_HEADS_DA):
            d = o[h * rows_h:h * rows_h + t, :] - lam * o[h * rows_h + t:(h + 1) * rows_h, :]
            o_ref[h] = _rmsnorm(d, gsub_ref[...]) * (1.0 - lam_init)


def _pages_per_step(n_pages):
    n_pp = min(PAGES_PER_STEP, n_pages)
    while n_pages % n_pp:
        n_pp -= 1
    return n_pp


def _ffn_units(x3, tt, f):
    n_ch = f // _tile(f, FFN_CHUNK)
    return x3.shape[0] * (x3.shape[1] // tt) * n_ch, n_ch, x3.shape[1] // tt


def _sample_attn(page_ids, q4, k_new4, v_new4, cache_kt, cache_vt, lam_vecs, gsub, lam_init, ffn=None):
    n_seq, n_heads, t, dh = q4.shape
    page = cache_kt.shape[-1]
    n_pages = page_ids.shape[0] // n_seq
    n_pp = _pages_per_step(n_pages)
    n_rows = 2 * n_heads * t
    assert (2 * t) % 16 == 0 and t <= LANES

    seq_spec = pl.BlockSpec((None, n_heads, t, dh), lambda s, c, pt: (s, 0, 0, 0))
    const = lambda shape: pl.BlockSpec(shape, lambda s, c, pt: (0,) * len(shape), pipeline_mode=pl.Buffered(1))
    n_chunks = n_pages // n_pp
    in_specs = ([const(lam_vecs.shape), const(gsub.shape), seq_spec, seq_spec, seq_spec]
                + [pl.BlockSpec(memory_space=pl.ANY)] * 2)
    out_specs = [seq_spec]
    out_shape = [jax.ShapeDtypeStruct((n_seq, n_heads, t, dh), F32)]
    scratch = [pltpu.VMEM((n_rows, dh), F32), pltpu.VMEM((n_rows, 1), F32),
               pltpu.VMEM((n_rows, 1), F32), pltpu.VMEM((n_rows, dh), F32),
               pltpu.VMEM((2, n_pp, n_heads, dh, page), F32), pltpu.VMEM((2, n_pp, n_heads, dh, page), F32),
               pltpu.SemaphoreType.DMA((2, 2))]
    operands = [page_ids, lam_vecs, gsub, q4, k_new4, v_new4, cache_kt, cache_vt]
    ffn_static = None
    if ffn is not None:
        x3, prev, gn, wup, cw, cb, wdn, tt = ffn
        b, _, d = x3.shape
        n_ch, fc, _ = wdn.shape
        n_units, _, tiles = _ffn_units(x3, tt, n_ch * fc)
        assert n_units <= n_seq * n_chunks
        ffn_static = (n_units, n_ch, tiles)
        prev_c = jnp.transpose(prev.reshape(b, FFN_CONV - 1, 2 * n_ch, fc), (0, 2, 1, 3))

        def tile_of(s, c):
            return jnp.minimum(s * n_chunks + c, n_units - 1) // n_ch

        x_spec = pl.BlockSpec((1, tt, d), lambda s, c, pt: (tile_of(s, c) // tiles, tile_of(s, c) % tiles, 0))
        st_spec = pl.BlockSpec((1, 2 * n_ch, FFN_CONV - 1, fc), lambda s, c, pt: (tile_of(s, c) // tiles, 0, 0, 0))
        ffn_in = [x3, prev_c, gn, wup, cw, cb, wdn]
        in_specs += [x_spec, st_spec] + [const(a.shape) for a in ffn_in[2:]]
        out_specs += [x_spec, st_spec]
        out_shape += [jax.ShapeDtypeStruct(x3.shape, F32), jax.ShapeDtypeStruct(prev_c.shape, F32)]
        scratch += [pltpu.VMEM((tt, d), BF16), pltpu.VMEM((2 * n_ch, 1, SUBLANES, fc), F32),
                    pltpu.VMEM((1, SUBLANES + tt, fc), F32)]
        operands += ffn_in
    grid_spec = pltpu.PrefetchScalarGridSpec(
        num_scalar_prefetch=1, grid=(n_seq, n_chunks), in_specs=in_specs, out_specs=out_specs, scratch_shapes=scratch)
    outs = pl.pallas_call(
        functools.partial(_sample_attn_kernel, n_pp=n_pp, t=t, past=n_pages * page, lam_init=lam_init,
                          ffn=ffn_static),
        grid_spec=grid_spec,
        out_shape=out_shape,
        compiler_params=pltpu.CompilerParams(dimension_semantics=("arbitrary",) * 2,
                                             vmem_limit_bytes=FUSED_VMEM_LIMIT if ffn is not None else VMEM_LIMIT),
        name="sample_attn",
    )(*operands)
    if ffn is None:
        return outs[0]
    state = jnp.transpose(outs[2], (0, 2, 1, 3)).reshape(prev.shape)
    return outs[0], outs[1], state


def _lru_kernel(lx_ref, lg_ref, cprev_ref, h0_ref, cw_ref, cb_ref, wa_ref, ba_ref, wx_ref, bx_ref, lam_ref,
                out_ref, hlast_ref, ext_scr, h_scr, a_scr, g_scr, *, g, tt):
    i = pl.program_id(1)
    w = LRU_WIDTH
    hist = LRU_CONV - 1
    top = SUBLANES

    @pl.when(i == 0)
    def _():
        ext_scr[:, top - hist:top, :] = cprev_ref[...]
        h_scr[...] = h0_ref[...]

    ext_scr[:, top:top + tt, :] = lx_ref[...]
    nc = tt // SUBLANES
    ext4 = ext_scr[...].reshape(g, nc + 1, SUBLANES, w)
    sub4 = lax.broadcasted_iota(jnp.int32, (g, nc, SUBLANES, w), 2)
    xc4 = cb_ref[...]
    for j in range(LRU_CONV):
        k = hist - j
        if k:
            rolled = pltpu.roll(ext4, k, 2)
            shifted = jnp.where(sub4 >= k, rolled[:, 1:], rolled[:, :nc])
        else:
            shifted = ext4[:, 1:]
        xc4 = xc4 + shifted * cw_ref[j:j + 1, :]
    ext_scr[:, top - hist:top, :] = ext_scr[:, top + tt - hist:top + tt, :]

    r = g * tt
    xc = xc4.reshape(r, w)
    xb = xc.astype(BF16)
    rg = _sigmoid(_dot(xb, wa_ref[...]) + ba_ref[...])
    ig = _sigmoid(_dot(xb, wx_ref[...]) + bx_ref[...])
    nl = -lam_ref[...]
    softplus = jnp.maximum(nl, 0.0) + jnp.log1p(jnp.exp(-jnp.abs(nl)))
    log_a = -LRU_C * rg * softplus
    a = jnp.exp(log_a)
    one_minus_a2 = -jnp.tanh(log_a) * (a * a + 1.0)
    gx = jnp.where(one_minus_a2 > 0.0, one_minus_a2 * lax.rsqrt(one_minus_a2), 0.0) * (ig * xc)

    nch = r // SUBLANES
    a3 = a.reshape(nch, SUBLANES, w)
    g3 = gx.reshape(nch, SUBLANES, w)
    sub = lax.broadcasted_iota(jnp.int32, (nch, SUBLANES, w), 1)
    for s in (1, 2, 4):
        keep = sub >= s
        a_prev = jnp.where(keep, pltpu.roll(a3, s, 1), 1.0)
        g_prev = jnp.where(keep, pltpu.roll(g3, s, 1), 0.0)
        g3 = a3 * g_prev + g3
        a3 = a3 * a_prev

    nc = tt // SUBLANES
    a_scr[...] = a3.reshape(g, nc, SUBLANES, w)
    g_scr[...] = g3.reshape(g, nc, SUBLANES, w)

    def chunk(c, h):
        hs = a_scr[:, c] * h + g_scr[:, c]
        g_scr[:, c] = hs
        return hs[:, SUBLANES - 1:SUBLANES, :]

    h = h_scr[...]
    if nc <= 2:
        for c in range(nc):
            h = chunk(c, h)
    else:
        h = lax.fori_loop(0, nc, chunk, h)
    h_scr[...] = h

    hs = g_scr[...].reshape(r, w)
    out = hs * jax.nn.gelu(lg_ref[...].reshape(r, w))
    out_ref[...] = out.reshape(g, tt, w).astype(out_ref.dtype)

    @pl.when(i == pl.num_programs(1) - 1)
    def _():
        hlast_ref[...] = h


def _lru(lx3, lg3, conv_prev, h0, cw, cb, wa, ba, wx, bx, lam, g, tt, out_dtype):
    n_seq, t, w = lx3.shape
    blk = pl.BlockSpec((g, tt, w), lambda s, i: (s, i, 0))
    seq = lambda rows: pl.BlockSpec((g, rows, w), lambda s, i: (s, 0, 0))
    return pl.pallas_call(
        functools.partial(_lru_kernel, g=g, tt=tt),
        grid=(n_seq // g, t // tt),
        in_specs=[blk, blk, seq(LRU_CONV - 1), seq(1)] + [_const_spec(a.shape) for a in (cw, cb, wa, ba, wx, bx, lam)],
        out_specs=[blk, seq(1)],
        out_shape=[jax.ShapeDtypeStruct((n_seq, t, w), out_dtype), jax.ShapeDtypeStruct((n_seq, 1, w), F32)],
        scratch_shapes=[pltpu.VMEM((g, SUBLANES + tt, w), F32), pltpu.VMEM((g, 1, w), F32),
                        pltpu.VMEM((g, tt // SUBLANES, SUBLANES, w), F32),
                        pltpu.VMEM((g, tt // SUBLANES, SUBLANES, w), F32)],
        compiler_params=_params(2),
        name="lru",
    )(lx3, lg3, conv_prev, h0, cw, cb, wa, ba, wx, bx, lam)


def _out_cross_kernel(x_ref, da_ref, lru_ref, mk_ref, mv_ref, wo_da_ref, wo_lru_ref, gn_ref, wcq_ref, gcq_ref,
                      wco_ref, out_ref, q_scr, o_scr, *, g, tt):
    r = g * tt
    d = x_ref.shape[-1]
    h = (x_ref[...].reshape(r, d)
         + _dot(da_ref[...].reshape(r, DA_WIDTH).astype(BF16), wo_da_ref[...])
         + _dot(lru_ref[...].reshape(r, LRU_WIDTH).astype(BF16), wo_lru_ref[...]))
    hn = _rmsnorm(h, gn_ref[...]).astype(BF16)
    qc = _dot(hn, wcq_ref[...])
    for hd in range(N_HEADS_X):
        sl = slice(hd * X_HEAD, (hd + 1) * X_HEAD)
        q_scr[:, sl] = _rmsnorm(qc[:, sl], gcq_ref[...])

    if mk_ref.shape[-1] == X_WIDTH:
        for hd in range(N_HEADS_X):
            sl = slice(hd * X_HEAD, (hd + 1) * X_HEAD)
            s = _dot_nt(q_scr[:, sl].astype(BF16), mk_ref[0, :, sl].astype(BF16))
            e = jnp.exp(s - jnp.max(s, axis=-1, keepdims=True))
            o = _dot(e.astype(BF16), mv_ref[0, :, sl].astype(BF16))
            o_scr[:, sl] = o / jnp.sum(e, axis=-1, keepdims=True)
    else:
        n_rows = N_HEADS_X * tt
        n_cols = mk_ref.shape[1]
        col_head = lax.broadcasted_iota(jnp.int32, (n_rows, n_cols), 1) % N_HEADS_X
        own_head = col_head == lax.broadcasted_iota(jnp.int32, (n_rows, n_cols), 0) // tt

        def one_seq(si, carry):
            r0 = pl.multiple_of(si * tt, tt)
            q = q_scr[pl.ds(r0, tt), :]
            qs = jnp.concatenate([q[:, hd * X_HEAD:(hd + 1) * X_HEAD] for hd in range(N_HEADS_X)], axis=0)
            s = jnp.where(own_head, _dot_nt(qs.astype(BF16), mk_ref[si].astype(BF16)), NEG)
            e = jnp.exp(s - jnp.max(s, axis=-1, keepdims=True))
            o = _dot(e.astype(BF16), mv_ref[si].astype(BF16)) / jnp.sum(e, axis=-1, keepdims=True)
            for hd in range(N_HEADS_X):
                o_scr[pl.ds(r0, tt), hd * X_HEAD:(hd + 1) * X_HEAD] = o[hd * tt:(hd + 1) * tt, :]
            return carry

        lax.fori_loop(0, g, one_seq, 0, unroll=4 if g % 4 == 0 else 1)

    out = h + _dot(o_scr[...].astype(BF16), wco_ref[...])
    out_ref[...] = out.reshape(g, tt, d)


def _out_cross(x3, da3, lru3, mk3, mv3, mem_first, wo_da, wo_lru, gn, wcq, gcq, wco, g, tt):
    n_seq, t, d = x3.shape
    assert mk3.shape[-1] == X_HEAD or (mk3.shape[-1] == X_WIDTH and g == 1)
    mem_blk0 = mem_first // g
    blk = lambda w: pl.BlockSpec((g, tt, w), lambda s, i: (s, i, 0))
    mem = pl.BlockSpec((g,) + mk3.shape[1:], lambda s, i: (s + mem_blk0, 0, 0))
    return pl.pallas_call(
        functools.partial(_out_cross_kernel, g=g, tt=tt),
        grid=(n_seq // g, t // tt),
        in_specs=[blk(d), blk(DA_WIDTH), blk(LRU_WIDTH), mem, mem]
                 + [_const_spec(a.shape) for a in (wo_da, wo_lru, gn, wcq, gcq, wco)],
        out_specs=blk(d),
        out_shape=jax.ShapeDtypeStruct((n_seq, t, d), F32),
        scratch_shapes=[pltpu.VMEM((g * tt, X_WIDTH), F32), pltpu.VMEM((g * tt, X_WIDTH), F32)],
        compiler_params=_params(2),
        name="out_cross",
    )(x3, da3, lru3, mk3, mv3, wo_da, wo_lru, gn, wcq, gcq, wco)


def _ffn_conv_half(xn, g, tt, up_w, conv_w, conv_b, h_ref, ext_scr):
    hist = FFN_CONV - 1
    top = SUBLANES
    fc = ext_scr.shape[-1]
    ext_scr[:, top:top + tt, :] = _dot(xn, up_w).reshape(g, tt, fc)
    ext_scr[:, top - hist:top, :] = h_ref[:, top - hist:top, :]
    c3 = conv_b
    for j in range(FFN_CONV):
        c3 = c3 + ext_scr[:, top - hist + j:top - hist + j + tt, :] * conv_w[j:j + 1, :]
    h_ref[:, top - hist:top, :] = ext_scr[:, top + tt - hist:top + tt, :]
    return c3.reshape(g * tt, fc)


def _ffn_gate(gate, value):
    return (gate * _sigmoid(gate) * value).astype(BF16)


def _conv_ffn_kernel(x_ref, prev_ref, gn_ref, wup_ref, cw_ref, cb_ref, wdn_ref, out_ref, state_ref,
                     halo_scr, ext_scr, *, g, tt):
    i = pl.program_id(1)
    r = g * tt
    d = x_ref.shape[-1]
    n_ch, fc, _ = wdn_ref.shape
    f = n_ch * fc
    hist = FFN_CONV - 1
    top = SUBLANES

    @pl.when(i == 0)
    def _():
        halo_scr[:, top - hist:top, :] = prev_ref[...]

    xn = _rmsnorm(x_ref[...].reshape(r, d), gn_ref[...]).astype(BF16)
    out_ref[...] = x_ref[...]

    for ch in range(n_ch):
        halves = []
        for half in range(2):
            j = half * n_ch + ch
            cols = slice(half * f + ch * fc, half * f + (ch + 1) * fc)
            halves.append(_ffn_conv_half(xn, g, tt, wup_ref[j], cw_ref[j], cb_ref[j], halo_scr.at[:, :, cols], ext_scr))
        out_ref[...] += _dot(_ffn_gate(*halves), wdn_ref[ch]).reshape(g, tt, d)

    @pl.when(i == pl.num_programs(1) - 1)
    def _():
        state_ref[...] = halo_scr[:, top - hist:top, :]


def _chunk_major(wup, cw, cb, wdn):
    f, d = wdn.shape
    fc = _tile(f, FFN_CHUNK)
    n_ch = f // fc
    chunked = lambda a: jnp.transpose(a.reshape(a.shape[0], 2, n_ch, fc), (1, 2, 0, 3)).reshape(
        2 * n_ch, a.shape[0], fc)
    return chunked(wup), chunked(cw), chunked(cb), wdn.reshape(n_ch, fc, d)


def _conv_ffn(x3, prev, gn, wup, cw, cb, wdn, g, tt):
    n_seq, t, d = x3.shape
    n_ch, fc, _ = wdn.shape
    f2 = 2 * n_ch * fc
    blk = pl.BlockSpec((g, tt, d), lambda s, i: (s, i, 0))
    st = pl.BlockSpec((g, FFN_CONV - 1, f2), lambda s, i: (s, 0, 0))
    return pl.pallas_call(
        functools.partial(_conv_ffn_kernel, g=g, tt=tt),
        grid=(n_seq // g, t // tt),
        in_specs=[blk, st] + [_const_spec(a.shape) for a in (gn, wup, cw, cb, wdn)],
        out_specs=[blk, st],
        out_shape=[jax.ShapeDtypeStruct((n_seq, t, d), F32), jax.ShapeDtypeStruct((n_seq, FFN_CONV - 1, f2), F32)],
        scratch_shapes=[pltpu.VMEM((g, SUBLANES, f2), F32), pltpu.VMEM((g, SUBLANES + tt, fc), F32)],
        compiler_params=_params(2),
        name="conv_ffn",
    )(x3, prev, gn, wup, cw, cb, wdn)


def _block_diag(blocks):
    n, a, b = blocks.shape
    eye = jnp.eye(n, dtype=blocks.dtype)
    return (eye[:, None, :, None] * blocks[:, :, None, :]).reshape(n * a, n * b)


def _layer_weights(l, norm_mix, w_in, g_q, g_k, lam_q1, lam_k1, lam_q2, lam_k2, g_sub, lru_conv_w, lru_conv_b,
                   lru_wa, lru_ba, lru_wx, lru_bx, lru_lambda, w_o, norm_cross, norm_mem, w_cq, w_ck, w_cv,
                   g_cq, g_ck, w_co, norm_ffn, w_up, ffn_conv_w, ffn_conv_b, w_down):
    row = lambda a: a[l].reshape(1, -1)
    n_maps = DA_WIDTH // DA_HALF
    return dict(
        norm_mix=row(norm_mix), w_in=w_in[l].astype(BF16),
        pmat=jnp.kron(jnp.eye(n_maps, dtype=F32), jnp.full((DA_HALF, DA_HALF), 1.0 / DA_HALF, F32)).astype(BF16),
        gq=jnp.tile(g_q[l] * (DA_HALF ** -0.5), n_maps).reshape(1, -1),
        gq_base2=jnp.tile(g_q[l] * (DA_HALF ** -0.5 * LOG2E), n_maps).reshape(1, -1),
        gk=jnp.tile(g_k[l], n_maps).reshape(1, -1),
        lam=jnp.stack([lam_q1[l], lam_k1[l], lam_q2[l], lam_k2[l]]),
        gsub2=jnp.tile(g_sub[l], LANES // DA_HEAD).reshape(1, -1),
        gsub=row(g_sub),
        cw=lru_conv_w[l], cb=row(lru_conv_b),
        wa=_block_diag(lru_wa[l]).astype(BF16), ba=row(lru_ba),
        wx=_block_diag(lru_wx[l]).astype(BF16), bx=row(lru_bx), lru_lambda=row(lru_lambda),
        wo_da=w_o[l, :DA_WIDTH].astype(BF16), wo_lru=w_o[l, DA_WIDTH:].astype(BF16),
        norm_cross=row(norm_cross), norm_mem=row(norm_mem),
        wcq=w_cq[l].astype(BF16), wck=w_ck[l].astype(BF16), wcv=w_cv[l].astype(BF16),
        gcq=(g_cq[l] * (X_HEAD ** -0.5)).reshape(1, -1), gck=row(g_ck), wco=w_co[l].astype(BF16),
        norm_ffn=row(norm_ffn),
        ffn=_chunk_major(w_up[l].astype(BF16), ffn_conv_w[l], row(ffn_conv_b), w_down[l].astype(BF16)),
    )


def _mix(x3, da3, lx, lg, conv_prev, h0, mk3, mv3, mem_first, p, g, tt, mid_dtype):
    n_seq, t, d = x3.shape
    assert t >= LRU_CONV - 1 and t >= FFN_CONV - 1
    lru3, h_last = _lru(lx.reshape(n_seq, t, LRU_WIDTH), lg.reshape(n_seq, t, LRU_WIDTH), conv_prev, h0,
                        p['cw'], p['cb'], p['wa'], p['ba'], p['wx'], p['bx'], p['lru_lambda'], g, tt, mid_dtype)
    g_x = g
    while g_x > 1 and 4 * g_x * mk3.shape[1] * mk3.shape[2] * 4 > MEM_BLOCK_BYTES:
        g_x //= 2
    h3 = _out_cross(x3, da3, lru3, mk3, mv3, mem_first, p['wo_da'], p['wo_lru'], p['norm_cross'], p['wcq'], p['gcq'],
                    p['wco'], g_x, tt)
    conv_state = lx.reshape(n_seq, t, LRU_WIDTH)[:, t - (LRU_CONV - 1):]
    return h3, conv_state, h_last.reshape(n_seq, LRU_WIDTH)


def kernel(x_prompt, x_sample, mem_prompt, cache_k, cache_v, page_table, cache_mem_k, cache_mem_v, state_lru_conv, state_lru_h, state_ffn_conv, norm_mix, w_in, g_q, g_k, lam_q1, lam_k1, lam_q2, lam_k2, g_sub, lru_conv_w, lru_conv_b, lru_wa, lru_ba, lru_wx, lru_bx, lru_lambda, w_o, norm_cross, norm_mem, w_cq, w_ck, w_cv, g_cq, g_ck, w_co, norm_ffn, w_up, ffn_conv_w, ffn_conv_b, w_down):
    depth = w_in.shape[0]
    b, t, d = x_prompt.shape
    nb, nt, _ = x_sample.shape
    n_mem = mem_prompt.shape[1]
    n_phys, page = cache_k.shape[1], cache_k.shape[2]
    f2 = w_up.shape[2]
    to_pages = lambda c: jnp.transpose(c, (0, 1, 3, 4, 2)).reshape(depth * n_phys, N_HEADS_DA, DA_HEAD, page)
    cache_kt, cache_vt = to_pages(cache_k), to_pages(cache_v)
    to_heads = lambda a, n, m: jnp.transpose(a.reshape(n, m, N_HEADS_DA, DA_HEAD), (0, 2, 1, 3))
    cmk = cache_mem_k.reshape(depth * nb, n_mem * N_HEADS_X, X_HEAD)
    cmv = cache_mem_v.reshape(depth * nb, n_mem * N_HEADS_X, X_HEAD)

    tt_p = _tile(t, ROW_TILE)
    g_s = SAMPLE_SEQS if nb % SAMPLE_SEQS == 0 else 1

    yp, ys = x_prompt, x_sample
    outs_p, outs_s = [], []
    for l in range(depth):
        p = _layer_weights(l, norm_mix, w_in, g_q, g_k, lam_q1, lam_k1, lam_q2, lam_k2, g_sub, lru_conv_w,
                           lru_conv_b, lru_wa, lru_ba, lru_wx, lru_bx, lru_lambda, w_o, norm_cross, norm_mem,
                           w_cq, w_ck, w_cv, g_cq, g_ck, w_co, norm_ffn, w_up, ffn_conv_w, ffn_conv_b, w_down)
        lam_init = 0.8 - 0.6 * math.exp(-0.3 * l)

        q, kt, vt, lx, lg = _in_proj(yp.reshape(b * t, d), p['norm_mix'], p['w_in'], p['pmat'], p['gq_base2'], p['gk'],
                                     BF16, seq_len=t)
        da = _prompt_attn(q, kt, vt, p['lam'], p['gsub2'], lam_init)
        from_t = lambda a: jnp.transpose(a.reshape(b, N_HEADS_DA, DA_HEAD, t), (0, 3, 1, 2))
        mk, mv = _mem_kv(mem_prompt.reshape(b * n_mem, d), p['norm_mem'], p['wck'], p['wcv'], p['gck'])
        hp, conv_state_p, h_last_p = _mix(
            yp, da.reshape(b, t, DA_WIDTH), lx, lg,
            jnp.zeros((b, LRU_CONV - 1, LRU_WIDTH), F32), jnp.zeros((b, 1, LRU_WIDTH), F32),
            mk.reshape(b, n_mem, X_WIDTH), mv.reshape(b, n_mem, X_WIDTH), 0, p, 1, tt_p, BF16)
        ffn_p = (hp, jnp.zeros((b, FFN_CONV - 1, f2), F32), p['norm_ffn']) + p['ffn']

        q, k, v, lx, lg = _in_proj(ys.reshape(nb * nt, d), p['norm_mix'], p['w_in'], p['pmat'], p['gq'], p['gk'], F32)
        page_ids = page_table.reshape(-1).astype(jnp.int32) + l * n_phys
        attn_args = (page_ids, to_heads(q, nb, nt), to_heads(k, nb, nt), to_heads(v, nb, nt),
                     cache_kt, cache_vt, p['lam'], p['gsub'], lam_init)
        n_pages = page_table.shape[1]
        if _ffn_units(hp, tt_p, f2 // 2)[0] <= nb * (n_pages // _pages_per_step(n_pages)):
            da, yp, ffn_state_p = _sample_attn(*attn_args, ffn=ffn_p + (tt_p,))
        else:
            da = _sample_attn(*attn_args)
            yp, ffn_state_p = _conv_ffn(*ffn_p, 1, tt_p)
        outs_p.append((from_t(kt), from_t(vt),
                       mk.reshape(b, n_mem, N_HEADS_X, X_HEAD), mv.reshape(b, n_mem, N_HEADS_X, X_HEAD),
                       conv_state_p, h_last_p, ffn_state_p))
        da = jnp.transpose(da, (0, 2, 1, 3))
        hs, conv_state, h_last = _mix(
            ys, da.reshape(nb, nt, DA_WIDTH), lx, lg,
            state_lru_conv[l], state_lru_h[l].reshape(nb, 1, LRU_WIDTH),
            cmk, cmv, l * nb, p, g_s, nt, F32)
        ys, ffn_state = _conv_ffn(hs, state_ffn_conv[l], p['norm_ffn'], *p['ffn'], g_s, nt)
        outs_s.append((k.reshape(nb, nt, N_HEADS_DA, DA_HEAD), v.reshape(nb, nt, N_HEADS_DA, DA_HEAD),
                       conv_state, h_last, ffn_state))

    stack = lambda outs, j: jnp.stack([o[j] for o in outs])
    return (yp, ys, stack(outs_p, 0), stack(outs_p, 1), stack(outs_s, 0), stack(outs_s, 1),
            stack(outs_p, 2), stack(outs_p, 3), stack(outs_p, 4), stack(outs_s, 2),
            stack(outs_p, 5), stack(outs_s, 3), stack(outs_p, 6), stack(outs_s, 4))
```

```python
import functools
import math

import jax
import jax.numpy as jnp
from jax import lax
from jax.experimental import pallas as pl
from jax.experimental.pallas import tpu as pltpu

F32 = jnp.float32
BF16 = jnp.bfloat16

EPS = 1e-6
N_HEADS_DA = 8
DA_HEAD = 64
DA_HALF = DA_HEAD // 2
DA_WIDTH = N_HEADS_DA * DA_HEAD
LRU_WIDTH = 512
LRU_CONV = 4
LRU_C = 8.0
N_HEADS_X = 4
X_HEAD = 128
X_WIDTH = N_HEADS_X * X_HEAD
FFN_CONV = 3

LANES = 128
SUBLANES = 8
NEG = -1e30
VMEM_LIMIT = 48 * 1024 * 1024

ROW_TILE = 512
Q_TILE = 256
PAGES_PER_STEP = 16
PAGE_SLOTS = 3
FFN_CHUNK = 256
FUSED_VMEM_LIMIT = 56 * 1024 * 1024
SAMPLE_SEQS = 32
MEM_BLOCK_BYTES = 16 * 1024 * 1024


def _tile(n, pref):
    t = min(n, pref)
    while n % t:
        t -= SUBLANES
    return t


def _const_spec(shape):
    zeros = (0,) * len(shape)
    return pl.BlockSpec(shape, lambda *_: zeros, pipeline_mode=pl.Buffered(1))


def _params(n_axes):
    return pltpu.CompilerParams(dimension_semantics=("arbitrary",) * n_axes, vmem_limit_bytes=VMEM_LIMIT)


def _rmsnorm(x, g):
    return x * lax.rsqrt(jnp.mean(x * x, axis=-1, keepdims=True) + EPS) * g


def _sigmoid(x):
    return 0.5 * jnp.tanh(0.5 * x) + 0.5


def _dot(a, b):
    return jnp.dot(a, b, preferred_element_type=F32)


def _dot_nt(a, b):
    return lax.dot_general(a, b, (((1,), (1,)), ((), ())), preferred_element_type=F32)


def _diff_lambda(lam_ref, lam_init):
    v = lam_ref[...]
    s1 = jnp.sum(v[0:1] * v[1:2], axis=-1, keepdims=True)
    s2 = jnp.sum(v[2:3] * v[3:4], axis=-1, keepdims=True)
    return jnp.exp(s1) - jnp.exp(s2) + lam_init


def _in_proj_kernel(x_ref, gn_ref, w_ref, p_ref, gq_ref, gk_ref, q_ref, k_ref, v_ref, lx_ref, lg_ref, *scratch):
    xn = _rmsnorm(x_ref[...], gn_ref[...]).astype(BF16)

    def proj(j):
        return _dot(xn, w_ref[:, j * DA_WIDTH:(j + 1) * DA_WIDTH])

    def map_norm(z, g):
        sq = z * z
        hi = sq.astype(BF16)
        lo = (sq - hi.astype(F32)).astype(BF16)
        ms = _dot(hi, p_ref[...]) + _dot(lo, p_ref[...])
        return z * lax.rsqrt(ms + EPS) * g

    q_ref[...] = map_norm(proj(0), gq_ref[...]).astype(q_ref.dtype)
    k = map_norm(proj(1), gk_ref[...])
    if scratch:
        v_scr, = scratch
        v_scr[...] = proj(2)
        k_ref[...] = k.T
        v_ref[...] = v_scr[...].T
    else:
        k_ref[...] = k
        v_ref[...] = proj(2)
    lx_ref[...] = proj(3)
    lg_ref[...] = proj(4)


def _in_proj(x2d, gn, w_in, pmat, gq, gk, q_dtype, seq_len=None):
    n, d = x2d.shape
    r = _tile(seq_len or n, ROW_TILE)
    row = lambda w: pl.BlockSpec((r, w), lambda i: (i, 0))
    out_sd = lambda dt: jax.ShapeDtypeStruct((n, DA_WIDTH), dt)
    kv_spec, kv_sd = row(DA_WIDTH), out_sd(F32)
    if seq_len:
        tiles = seq_len // r
        kv_spec = pl.BlockSpec((None, DA_WIDTH, r), lambda i: (i // tiles, 0, i % tiles))
        kv_sd = jax.ShapeDtypeStruct((n // seq_len, DA_WIDTH, seq_len), F32)
    return pl.pallas_call(
        _in_proj_kernel,
        grid=(n // r,),
        in_specs=[row(d), _const_spec(gn.shape), _const_spec(w_in.shape), _const_spec(pmat.shape),
                  _const_spec(gq.shape), _const_spec(gk.shape)],
        out_specs=[row(DA_WIDTH), kv_spec, kv_spec, row(DA_WIDTH), row(DA_WIDTH)],
        out_shape=[out_sd(q_dtype), kv_sd, kv_sd, out_sd(F32), out_sd(F32)],
        scratch_shapes=[pltpu.VMEM((r, DA_WIDTH), F32)] if seq_len else [],
        compiler_params=_params(1),
        name="in_proj",
    )(x2d, gn, w_in, pmat, gq, gk)


def _mem_kv_kernel(m_ref, gn_ref, wk_ref, wv_ref, gck_ref, k_ref, v_ref):
    mn = _rmsnorm(m_ref[...], gn_ref[...]).astype(BF16)
    k = _dot(mn, wk_ref[...])
    for h in range(N_HEADS_X):
        sl = slice(h * X_HEAD, (h + 1) * X_HEAD)
        k_ref[:, sl] = _rmsnorm(k[:, sl], gck_ref[...])
    v_ref[...] = _dot(mn, wv_ref[...])


def _mem_kv(mem2d, gn, w_ck, w_cv, g_ck):
    n, d = mem2d.shape
    r = _tile(n, ROW_TILE)
    row = lambda w: pl.BlockSpec((r, w), lambda i: (i, 0))
    return pl.pallas_call(
        _mem_kv_kernel,
        grid=(n // r,),
        in_specs=[row(d), _const_spec(gn.shape), _const_spec(w_ck.shape), _const_spec(w_cv.shape),
                  _const_spec(g_ck.shape)],
        out_specs=[row(X_WIDTH)] * 2,
        out_shape=[jax.ShapeDtypeStruct((n, X_WIDTH), F32)] * 2,
        compiler_params=_params(1),
        name="mem_kv",
    )(mem2d, gn, w_ck, w_cv, g_ck)


def _head_slope(head, shape):
    e = (head + 1).astype(F32) * (-8.0 / N_HEADS_DA)
    return jnp.exp2(jnp.full(shape, e, F32))


def _sub_norm_pair(o, lane, gsub, lam_init):
    sq = o * o
    first = lane < DA_HEAD
    ms0 = jnp.sum(jnp.where(first, sq, 0.0), axis=-1, keepdims=True)
    ms1 = jnp.sum(jnp.where(first, 0.0, sq), axis=-1, keepdims=True)
    ms = jnp.where(first, ms0, ms1) * (1.0 / DA_HEAD)
    return (o * lax.rsqrt(ms + EPS) * gsub) * (1.0 - lam_init)


BIAS_SPLIT = 256


N_BIAS_TERMS = 3
LOG2E = math.log2(math.e)


def _prompt_attn_kernel(lam_ref, gsub_ref, q_ref, kt_ref, vt_ref, o_ref, kb_scr, vb_scr, *, tq, lam_init):
    hp = pl.program_id(1)
    t = kt_ref.shape[1]
    other = lambda hh: (1 - hh) * DA_HEAD

    slab_row = lax.broadcasted_iota(jnp.int32, (LANES, t), 0)
    pos = lax.broadcasted_iota(jnp.int32, (LANES, t), 1)
    pos_hi = (pos // BIAS_SPLIT * BIAS_SPLIT).astype(F32)
    pos_lo = (pos % BIAS_SPLIT).astype(F32)
    for hh in range(2):
        own = (slab_row >= hh * DA_HEAD) & (slab_row < (hh + 1) * DA_HEAD)
        rel = slab_row - other(hh)
        in_bias = (rel >= 0) & (rel < 2 * N_BIAS_TERMS)
        bias = jnp.where(in_bias, jnp.where(rel % 2 == 0, pos_hi, pos_lo), 0.0)
        kb_scr[hh] = jnp.where(own, kt_ref[...], bias).astype(BF16)
        vb_scr[hh] = jnp.where(own, vt_ref[...], 1.0).astype(BF16)

    lane = lax.broadcasted_iota(jnp.int32, (tq, LANES), 1)
    visible = (lax.broadcasted_iota(jnp.int32, (tq, tq), 1) <= lax.broadcasted_iota(jnp.int32, (tq, tq), 0))
    lam = _diff_lambda(lam_ref, lam_init)

    def q_slab(q, hh, c):
        lo = hh * DA_HEAD + c * DA_HALF
        rel = lane - other(hh)
        rest = _head_slope(hp * 2 + hh, (1, LANES)) * LOG2E
        factor = jnp.zeros((tq, LANES), F32)
        for piece in range(N_BIAS_TERMS):
            part = rest.astype(BF16).astype(F32)
            factor = jnp.where((rel >= 2 * piece) & (rel < 2 * piece + 2), part, factor)
            rest = rest - part
        return jnp.where((lane >= lo) & (lane < lo + DA_HALF), q, factor).astype(BF16)

    for qi in range(t // tq):
        rows = slice(qi * tq, (qi + 1) * tq)
        past = qi * tq
        q = q_ref[rows, :].astype(F32)
        heads = []
        for hh in range(2):
            own = (lane >= hh * DA_HEAD) & (lane < (hh + 1) * DA_HEAD)
            maps = []
            for c in range(2):
                qm = q_slab(q, hh, c)
                s_diag = jnp.where(visible, _dot(qm, kb_scr[hh, :, rows]), NEG)
                m = jnp.max(s_diag, axis=-1, keepdims=True)
                if past:
                    s_past = _dot(qm, kb_scr[hh, :, :past])
                    m = jnp.maximum(m, jnp.max(s_past, axis=-1, keepdims=True))
                acc = _dot_nt(jnp.exp2(s_diag - m).astype(BF16), vb_scr[hh, :, rows])
                if past:
                    acc = acc + _dot_nt(jnp.exp2(s_past - m).astype(BF16), vb_scr[hh, :, :past])
                maps.append(acc / jnp.where(own, pltpu.roll(acc, DA_HEAD, 1), 1.0))
            heads.append(maps[0] - lam * maps[1])
        o = jnp.where(lane < DA_HEAD, heads[0], heads[1])
        o_ref[rows, :] = _sub_norm_pair(o, lane, gsub_ref[...], lam_init).astype(o_ref.dtype)


def _prompt_attn(q, kt, vt, lam_vecs, gsub2, lam_init):
    b, _, t = kt.shape
    tq = _tile(t, Q_TILE)
    n_pairs = DA_WIDTH // LANES
    assert t <= BIAS_SPLIT * 256
    spec = pl.BlockSpec((t, LANES), lambda bi, hp: (bi, hp))
    spec_t = pl.BlockSpec((None, LANES, t), lambda bi, hp: (bi, hp, 0))
    return pl.pallas_call(
        functools.partial(_prompt_attn_kernel, tq=tq, lam_init=lam_init),
        grid=(b, n_pairs),
        in_specs=[_const_spec(lam_vecs.shape), _const_spec(gsub2.shape), spec, spec_t, spec_t],
        out_specs=spec,
        out_shape=jax.ShapeDtypeStruct((b * t, DA_WIDTH), BF16),
        scratch_shapes=[pltpu.VMEM((2, LANES, t), BF16), pltpu.VMEM((2, LANES, t), BF16)],
        compiler_params=_params(2),
        name="prompt_attn",
    )(lam_vecs, gsub2, q, kt, vt)


def _unit_of_step(step, n_units, n_steps):
    return (step * n_units) // n_steps


def _ffn_unit(refs, n_units, n_ch, tiles_per_seq, n_steps):
    (x_ref, prev_ref, gn_ref, wup_ref, cw_ref, cb_ref, wdn_ref, y_ref, state_ref, xn_scr, halo_scr, ext_scr) = refs
    hist = FFN_CONV - 1
    top = SUBLANES
    rows, d = x_ref.shape[1], x_ref.shape[2]
    step = pl.program_id(0) * pl.num_programs(1) + pl.program_id(1)
    ua = _unit_of_step(step, n_units, n_steps)
    active = _unit_of_step(step + 1, n_units, n_steps) > ua
    rt = ua // n_ch
    ch = ua % n_ch
    first_chunk = active & (ch == 0)

    @pl.when(first_chunk & (rt % tiles_per_seq == 0))
    def _():
        halo_scr[:, 0, top - hist:top, :] = prev_ref[0]

    @pl.when(first_chunk)
    def _():
        xn_scr[...] = _rmsnorm(x_ref[0], gn_ref[...]).astype(BF16)
        y_ref[...] = x_ref[...]

    halves = []

    def conv_half(half):
        j = half * n_ch + ch
        halves.append(_ffn_conv_half(xn_scr[...], 1, rows, wup_ref[j], cw_ref[j], cb_ref[j], halo_scr.at[j], ext_scr))

    def down():
        y_ref[...] += _dot(_ffn_gate(*halves), wdn_ref[ch]).reshape(1, rows, d)

    def finish():
        @pl.when(active & (rt % tiles_per_seq == tiles_per_seq - 1))
        def _():
            for h in range(2):
                state_ref[0, h * n_ch + ch] = halo_scr[h * n_ch + ch, 0, top - hist:top, :]

    return active, (functools.partial(conv_half, 0), functools.partial(conv_half, 1), down), finish


def _sample_attn_kernel(pt_ref, lam_ref, gsub_ref, q_ref, kn_ref, vn_ref, kc_ref, vc_ref, *refs,
                        n_pp, t, past, lam_init, ffn):
    no_stages = (lambda: None,) * 3
    if ffn:
        o_ref = refs[7]
        q_scr, m_scr, l_scr, acc_scr, kbuf, vbuf, sem = refs[10:17]
        ffn_active, ffn_stages, ffn_finish = _ffn_unit(refs[:7] + refs[8:10] + refs[17:], *ffn)
    else:
        o_ref, q_scr, m_scr, l_scr, acc_scr, kbuf, vbuf, sem = refs
    ci = pl.program_id(1)

    step = pl.program_id(0) * pl.num_programs(1) + ci
    n_steps = pl.num_programs(0) * pl.num_programs(1)
    ahead = PAGE_SLOTS - 1
    slot = step % PAGE_SLOTS

    def page_copies(of_step):
        into_slot = of_step % PAGE_SLOTS
        copies = []
        for i in range(n_pp):
            pid = pt_ref[of_step * n_pp + i]
            copies.append(pltpu.make_async_copy(kc_ref.at[pid], kbuf.at[into_slot, i], sem.at[into_slot, 0]))
            copies.append(pltpu.make_async_copy(vc_ref.at[pid], vbuf.at[into_slot, i], sem.at[into_slot, 1]))
        return copies

    for first in range(ahead):
        @pl.when((step == 0) & (first < n_steps))
        def _(first=first):
            for cp in page_copies(first):
                cp.start()

    @pl.when(step + ahead < n_steps)
    def _():
        for cp in page_copies(step + ahead):
            cp.start()

    for cp in page_copies(step):
        cp.wait()
    k_pages = [kbuf.at[slot, i] for i in range(n_pp)]
    v_pages = [vbuf.at[slot, i] for i in range(n_pp)]
    rows_h = 2 * t
    n_rows = N_HEADS_DA * rows_h
    tk = n_pp * kbuf.shape[-1]
    head_rows = lambda h: slice(h * rows_h, (h + 1) * rows_h)

    @pl.when(ci == 0)
    def _():
        first_map = lax.broadcasted_iota(jnp.int32, (t, DA_HEAD), 1) < DA_HALF
        for h in range(N_HEADS_DA):
            qh = q_ref[h]
            q_scr[h * rows_h:h * rows_h + t, :] = jnp.where(first_map, qh, 0.0)
            q_scr[h * rows_h + t:(h + 1) * rows_h, :] = jnp.where(first_map, 0.0, qh)
        m_scr[...] = jnp.full(m_scr.shape, NEG, F32)
        l_scr[...] = jnp.zeros(l_scr.shape, F32)
        acc_scr[...] = jnp.zeros(acc_scr.shape, F32)

    r_idx = lax.broadcasted_iota(jnp.int32, (n_rows, 1), 0)
    slope = jnp.exp2((r_idx // rows_h + 1).astype(F32) * (-8.0 / N_HEADS_DA))
    q_pos = r_idx % t

    def scores(keys_t):
        return jnp.concatenate(
            [_dot(q_scr[head_rows(h), :].astype(BF16), keys_t(h)) for h in range(N_HEADS_DA)], axis=0)

    def update(s, weighted_values, between=lambda: None):
        m_old = m_scr[...]
        m_new = jnp.maximum(m_old, jnp.max(s, axis=-1, keepdims=True))
        alpha = jnp.exp(m_old - m_new)
        p = jnp.exp(s - m_new)
        l_scr[...] = alpha * l_scr[...] + jnp.sum(p, axis=-1, keepdims=True)
        pb = p.astype(BF16)
        between()
        pv = jnp.concatenate([weighted_values(h, pb[head_rows(h), :]) for h in range(N_HEADS_DA)], axis=0)
        acc_scr[...] = alpha * acc_scr[...] + pv
        m_scr[...] = m_new

    def cached_pages(stages):
        key_off = lax.broadcasted_iota(jnp.int32, (1, tk), 1) + (ci * tk - past)
        s = scores(lambda h: jnp.concatenate([kp[h] for kp in k_pages], axis=1).astype(BF16))
        s = s - slope * (q_pos - key_off).astype(F32)
        stages[0]()
        update(s, lambda h, ph: _dot_nt(ph, jnp.concatenate([vp[h] for vp in v_pages], axis=1).astype(BF16)),
               between=stages[1])
        stages[2]()

    if ffn:
        pl.when(ffn_active)(lambda: cached_pages(ffn_stages))
        pl.when(jnp.logical_not(ffn_active))(lambda: cached_pages(no_stages))
        ffn_finish()
    else:
        cached_pages(no_stages)

    @pl.when(ci == pl.num_programs(1) - 1)
    def _():
        pad = jnp.zeros((LANES - t, DA_HEAD), F32)
        padded = lambda ref, h: jnp.concatenate([ref[h], pad], axis=0).astype(BF16)
        new_i = lax.broadcasted_iota(jnp.int32, (1, LANES), 1)
        s = jnp.concatenate(
            [_dot_nt(q_scr[head_rows(h), :].astype(BF16), padded(kn_ref, h)) for h in range(N_HEADS_DA)], axis=0)
        s = s - slope * (q_pos - new_i).astype(F32)
        s = jnp.where((new_i < t) & (new_i <= q_pos), s, NEG)
        update(s, lambda h, ph: _dot(ph, padded(vn_ref, h)))

        o = acc_scr[...] / l_scr[...]
        lam = _diff_lambda(lam_ref, lam_init)
        for h in range(N_HEADS_DA):
            d = o[h * rows_h:h * rows_h + t, :] - lam * o[h * rows_h + t:(h + 1) * rows_h, :]
            o_ref[h] = _rmsnorm(d, gsub_ref[...]) * (1.0 - lam_init)


def _pages_per_step(n_pages):
    n_pp = min(PAGES_PER_STEP, n_pages)
    while n_pages % n_pp:
        n_pp -= 1
    return n_pp


def _ffn_units(x3, tt, f):
    n_ch = f // _tile(f, FFN_CHUNK)
    return x3.shape[0] * (x3.shape[1] // tt) * n_ch, n_ch, x3.shape[1] // tt


def _sample_attn(page_ids, q4, k_new4, v_new4, cache_kt, cache_vt, lam_vecs, gsub, lam_init, ffn=None):
    n_seq, n_heads, t, dh = q4.shape
    page = cache_kt.shape[-1]
    n_pages = page_ids.shape[0] // n_seq
    n_pp = _pages_per_step(n_pages)
    n_rows = 2 * n_heads * t
    assert (2 * t) % 16 == 0 and t <= LANES

    seq_spec = pl.BlockSpec((None, n_heads, t, dh), lambda s, c, pt: (s, 0, 0, 0))
    const = lambda shape: pl.BlockSpec(shape, lambda s, c, pt: (0,) * len(shape), pipeline_mode=pl.Buffered(1))
    n_chunks = n_pages // n_pp
    in_specs = ([const(lam_vecs.shape), const(gsub.shape), seq_spec, seq_spec, seq_spec]
                + [pl.BlockSpec(memory_space=pl.ANY)] * 2)
    out_specs = [seq_spec]
    out_shape = [jax.ShapeDtypeStruct((n_seq, n_heads, t, dh), F32)]
    scratch = [pltpu.VMEM((n_rows, dh), F32), pltpu.VMEM((n_rows, 1), F32),
               pltpu.VMEM((n_rows, 1), F32), pltpu.VMEM((n_rows, dh), F32),
               pltpu.VMEM((PAGE_SLOTS, n_pp, n_heads, dh, page), F32),
               pltpu.VMEM((PAGE_SLOTS, n_pp, n_heads, dh, page), F32), pltpu.SemaphoreType.DMA((PAGE_SLOTS, 2))]
    operands = [page_ids, lam_vecs, gsub, q4, k_new4, v_new4, cache_kt, cache_vt]
    ffn_static = None
    if ffn is not None:
        x3, prev, gn, wup, cw, cb, wdn, tt = ffn
        b, _, d = x3.shape
        n_ch, fc, _ = wdn.shape
        n_units, _, tiles = _ffn_units(x3, tt, n_ch * fc)
        n_steps = n_seq * n_chunks
        assert n_units <= n_steps
        ffn_static = (n_units, n_ch, tiles, n_steps)
        prev_c = jnp.transpose(prev.reshape(b, FFN_CONV - 1, 2 * n_ch, fc), (0, 2, 1, 3))

        def tile_of(s, c):
            return _unit_of_step(s * n_chunks + c, n_units, n_steps) // n_ch

        x_spec = pl.BlockSpec((1, tt, d), lambda s, c, pt: (tile_of(s, c) // tiles, tile_of(s, c) % tiles, 0))
        st_spec = pl.BlockSpec((1, 2 * n_ch, FFN_CONV - 1, fc), lambda s, c, pt: (tile_of(s, c) // tiles, 0, 0, 0))
        ffn_in = [x3, prev_c, gn, wup, cw, cb, wdn]
        in_specs += [x_spec, st_spec] + [const(a.shape) for a in ffn_in[2:]]
        out_specs += [x_spec, st_spec]
        out_shape += [jax.ShapeDtypeStruct(x3.shape, F32), jax.ShapeDtypeStruct(prev_c.shape, F32)]
        scratch += [pltpu.VMEM((tt, d), BF16), pltpu.VMEM((2 * n_ch, 1, SUBLANES, fc), F32),
                    pltpu.VMEM((1, SUBLANES + tt, fc), F32)]
        operands += ffn_in
    grid_spec = pltpu.PrefetchScalarGridSpec(
        num_scalar_prefetch=1, grid=(n_seq, n_chunks), in_specs=in_specs, out_specs=out_specs, scratch_shapes=scratch)
    outs = pl.pallas_call(
        functools.partial(_sample_attn_kernel, n_pp=n_pp, t=t, past=n_pages * page, lam_init=lam_init,
                          ffn=ffn_static),
        grid_spec=grid_spec,
        out_shape=out_shape,
        compiler_params=pltpu.CompilerParams(dimension_semantics=("arbitrary",) * 2,
                                             vmem_limit_bytes=FUSED_VMEM_LIMIT if ffn is not None else VMEM_LIMIT),
        name="sample_attn",
    )(*operands)
    if ffn is None:
        return outs[0]
    state = jnp.transpose(outs[2], (0, 2, 1, 3)).reshape(prev.shape)
    return outs[0], outs[1], state


def _lru_kernel(lx_ref, lg_ref, cprev_ref, h0_ref, cw_ref, cb_ref, wa_ref, ba_ref, wx_ref, bx_ref, lam_ref,
                out_ref, hlast_ref, ext_scr, h_scr, a_scr, g_scr, *, g, tt):
    i = pl.program_id(1)
    w = LRU_WIDTH
    hist = LRU_CONV - 1
    top = SUBLANES

    @pl.when(i == 0)
    def _():
        ext_scr[:, top - hist:top, :] = cprev_ref[...]
        h_scr[...] = h0_ref[...]

    ext_scr[:, top:top + tt, :] = lx_ref[...]
    nc = tt // SUBLANES
    ext4 = ext_scr[...].reshape(g, nc + 1, SUBLANES, w)
    sub4 = lax.broadcasted_iota(jnp.int32, (g, nc, SUBLANES, w), 2)
    xc4 = cb_ref[...]
    for j in range(LRU_CONV):
        k = hist - j
        if k:
            rolled = pltpu.roll(ext4, k, 2)
            shifted = jnp.where(sub4 >= k, rolled[:, 1:], rolled[:, :nc])
        else:
            shifted = ext4[:, 1:]
        xc4 = xc4 + shifted * cw_ref[j:j + 1, :]
    ext_scr[:, top - hist:top, :] = ext_scr[:, top + tt - hist:top + tt, :]

    r = g * tt
    xc = xc4.reshape(r, w)
    xb = xc.astype(BF16)
    rg = _sigmoid(_dot(xb, wa_ref[...]) + ba_ref[...])
    ig = _sigmoid(_dot(xb, wx_ref[...]) + bx_ref[...])
    nl = -lam_ref[...]
    softplus = jnp.maximum(nl, 0.0) + jnp.log1p(jnp.exp(-jnp.abs(nl)))
    log_a = -LRU_C * rg * softplus
    a = jnp.exp(log_a)
    one_minus_a2 = -jnp.tanh(log_a) * (a * a + 1.0)
    gx = jnp.where(one_minus_a2 > 0.0, one_minus_a2 * lax.rsqrt(one_minus_a2), 0.0) * (ig * xc)

    nch = r // SUBLANES
    a3 = a.reshape(nch, SUBLANES, w)
    g3 = gx.reshape(nch, SUBLANES, w)
    sub = lax.broadcasted_iota(jnp.int32, (nch, SUBLANES, w), 1)
    for s in (1, 2, 4):
        keep = sub >= s
        a_prev = jnp.where(keep, pltpu.roll(a3, s, 1), 1.0)
        g_prev = jnp.where(keep, pltpu.roll(g3, s, 1), 0.0)
        g3 = a3 * g_prev + g3
        a3 = a3 * a_prev

    nc = tt // SUBLANES
    a_scr[...] = a3.reshape(g, nc, SUBLANES, w)
    g_scr[...] = g3.reshape(g, nc, SUBLANES, w)

    def chunk(c, h):
        hs = a_scr[:, c] * h + g_scr[:, c]
        g_scr[:, c] = hs
        return hs[:, SUBLANES - 1:SUBLANES, :]

    h = h_scr[...]
    if nc <= 2:
        for c in range(nc):
            h = chunk(c, h)
    else:
        h = lax.fori_loop(0, nc, chunk, h)
    h_scr[...] = h

    hs = g_scr[...].reshape(r, w)
    out = hs * jax.nn.gelu(lg_ref[...].reshape(r, w))
    out_ref[...] = out.reshape(g, tt, w).astype(out_ref.dtype)

    @pl.when(i == pl.num_programs(1) - 1)
    def _():
        hlast_ref[...] = h


def _lru(lx3, lg3, conv_prev, h0, cw, cb, wa, ba, wx, bx, lam, g, tt, out_dtype):
    n_seq, t, w = lx3.shape
    blk = pl.BlockSpec((g, tt, w), lambda s, i: (s, i, 0))
    seq = lambda rows: pl.BlockSpec((g, rows, w), lambda s, i: (s, 0, 0))
    return pl.pallas_call(
        functools.partial(_lru_kernel, g=g, tt=tt),
        grid=(n_seq // g, t // tt),
        in_specs=[blk, blk, seq(LRU_CONV - 1), seq(1)] + [_const_spec(a.shape) for a in (cw, cb, wa, ba, wx, bx, lam)],
        out_specs=[blk, seq(1)],
        out_shape=[jax.ShapeDtypeStruct((n_seq, t, w), out_dtype), jax.ShapeDtypeStruct((n_seq, 1, w), F32)],
        scratch_shapes=[pltpu.VMEM((g, SUBLANES + tt, w), F32), pltpu.VMEM((g, 1, w), F32),
                        pltpu.VMEM((g, tt // SUBLANES, SUBLANES, w), F32),
                        pltpu.VMEM((g, tt // SUBLANES, SUBLANES, w), F32)],
        compiler_params=_params(2),
        name="lru",
    )(lx3, lg3, conv_prev, h0, cw, cb, wa, ba, wx, bx, lam)


def _out_cross_kernel(x_ref, da_ref, lru_ref, mk_ref, mv_ref, wo_da_ref, wo_lru_ref, gn_ref, wcq_ref, gcq_ref,
                      wco_ref, out_ref, q_scr, o_scr, *, g, tt):
    r = g * tt
    d = x_ref.shape[-1]
    h = (x_ref[...].reshape(r, d)
         + _dot(da_ref[...].reshape(r, DA_WIDTH).astype(BF16), wo_da_ref[...])
         + _dot(lru_ref[...].reshape(r, LRU_WIDTH).astype(BF16), wo_lru_ref[...]))
    hn = _rmsnorm(h, gn_ref[...]).astype(BF16)
    qc = _dot(hn, wcq_ref[...])
    for hd in range(N_HEADS_X):
        sl = slice(hd * X_HEAD, (hd + 1) * X_HEAD)
        q_scr[:, sl] = _rmsnorm(qc[:, sl], gcq_ref[...])

    if mk_ref.shape[-1] == X_WIDTH:
        for hd in range(N_HEADS_X):
            sl = slice(hd * X_HEAD, (hd + 1) * X_HEAD)
            s = _dot_nt(q_scr[:, sl].astype(BF16), mk_ref[0, :, sl].astype(BF16))
            e = jnp.exp(s - jnp.max(s, axis=-1, keepdims=True))
            o = _dot(e.astype(BF16), mv_ref[0, :, sl].astype(BF16))
            o_scr[:, sl] = o / jnp.sum(e, axis=-1, keepdims=True)
    else:
        n_rows = N_HEADS_X * tt
        n_cols = mk_ref.shape[1]
        col_head = lax.broadcasted_iota(jnp.int32, (n_rows, n_cols), 1) % N_HEADS_X
        own_head = col_head == lax.broadcasted_iota(jnp.int32, (n_rows, n_cols), 0) // tt

        def one_seq(si, carry):
            r0 = pl.multiple_of(si * tt, tt)
            q = q_scr[pl.ds(r0, tt), :]
            qs = jnp.concatenate([q[:, hd * X_HEAD:(hd + 1) * X_HEAD] for hd in range(N_HEADS_X)], axis=0)
            s = jnp.where(own_head, _dot_nt(qs.astype(BF16), mk_ref[si].astype(BF16)), NEG)
            e = jnp.exp(s - jnp.max(s, axis=-1, keepdims=True))
            o = _dot(e.astype(BF16), mv_ref[si].astype(BF16)) / jnp.sum(e, axis=-1, keepdims=True)
            for hd in range(N_HEADS_X):
                o_scr[pl.ds(r0, tt), hd * X_HEAD:(hd + 1) * X_HEAD] = o[hd * tt:(hd + 1) * tt, :]
            return carry

        lax.fori_loop(0, g, one_seq, 0, unroll=4 if g % 4 == 0 else 1)

    out = h + _dot(o_scr[...].astype(BF16), wco_ref[...])
    out_ref[...] = out.reshape(g, tt, d)


def _out_cross(x3, da3, lru3, mk3, mv3, mem_first, wo_da, wo_lru, gn, wcq, gcq, wco, g, tt):
    n_seq, t, d = x3.shape
    assert mk3.shape[-1] == X_HEAD or (mk3.shape[-1] == X_WIDTH and g == 1)
    mem_blk0 = mem_first // g
    blk = lambda w: pl.BlockSpec((g, tt, w), lambda s, i: (s, i, 0))
    mem = pl.BlockSpec((g,) + mk3.shape[1:], lambda s, i: (s + mem_blk0, 0, 0))
    return pl.pallas_call(
        functools.partial(_out_cross_kernel, g=g, tt=tt),
        grid=(n_seq // g, t // tt),
        in_specs=[blk(d), blk(DA_WIDTH), blk(LRU_WIDTH), mem, mem]
                 + [_const_spec(a.shape) for a in (wo_da, wo_lru, gn, wcq, gcq, wco)],
        out_specs=blk(d),
        out_shape=jax.ShapeDtypeStruct((n_seq, t, d), F32),
        scratch_shapes=[pltpu.VMEM((g * tt, X_WIDTH), F32), pltpu.VMEM((g * tt, X_WIDTH), F32)],
        compiler_params=_params(2),
        name="out_cross",
    )(x3, da3, lru3, mk3, mv3, wo_da, wo_lru, gn, wcq, gcq, wco)


def _ffn_conv_half(xn, g, tt, up_w, conv_w, conv_b, h_ref, ext_scr):
    hist = FFN_CONV - 1
    top = SUBLANES
    fc = ext_scr.shape[-1]
    ext_scr[:, top:top + tt, :] = _dot(xn, up_w).reshape(g, tt, fc)
    ext_scr[:, top - hist:top, :] = h_ref[:, top - hist:top, :]
    c3 = conv_b
    for j in range(FFN_CONV):
        c3 = c3 + ext_scr[:, top - hist + j:top - hist + j + tt, :] * conv_w[j:j + 1, :]
    h_ref[:, top - hist:top, :] = ext_scr[:, top + tt - hist:top + tt, :]
    return c3.reshape(g * tt, fc)


def _ffn_gate(gate, value):
    return (gate * _sigmoid(gate) * value).astype(BF16)


def _conv_ffn_kernel(x_ref, prev_ref, gn_ref, wup_ref, cw_ref, cb_ref, wdn_ref, out_ref, state_ref,
                     halo_scr, ext_scr, *, g, tt):
    i = pl.program_id(1)
    r = g * tt
    d = x_ref.shape[-1]
    n_ch, fc, _ = wdn_ref.shape
    f = n_ch * fc
    hist = FFN_CONV - 1
    top = SUBLANES

    @pl.when(i == 0)
    def _():
        halo_scr[:, top - hist:top, :] = prev_ref[...]

    xn = _rmsnorm(x_ref[...].reshape(r, d), gn_ref[...]).astype(BF16)
    out_ref[...] = x_ref[...]

    for ch in range(n_ch):
        halves = []
        for half in range(2):
            j = half * n_ch + ch
            cols = slice(half * f + ch * fc, half * f + (ch + 1) * fc)
            halves.append(_ffn_conv_half(xn, g, tt, wup_ref[j], cw_ref[j], cb_ref[j], halo_scr.at[:, :, cols], ext_scr))
        out_ref[...] += _dot(_ffn_gate(*halves), wdn_ref[ch]).reshape(g, tt, d)

    @pl.when(i == pl.num_programs(1) - 1)
    def _():
        state_ref[...] = halo_scr[:, top - hist:top, :]


def _chunk_major(wup, cw, cb, wdn):
    f, d = wdn.shape
    fc = _tile(f, FFN_CHUNK)
    n_ch = f // fc
    chunked = lambda a: jnp.transpose(a.reshape(a.shape[0], 2, n_ch, fc), (1, 2, 0, 3)).reshape(
        2 * n_ch, a.shape[0], fc)
    return chunked(wup), chunked(cw), chunked(cb), wdn.reshape(n_ch, fc, d)


def _conv_ffn(x3, prev, gn, wup, cw, cb, wdn, g, tt):
    n_seq, t, d = x3.shape
    n_ch, fc, _ = wdn.shape
    f2 = 2 * n_ch * fc
    blk = pl.BlockSpec((g, tt, d), lambda s, i: (s, i, 0))
    st = pl.BlockSpec((g, FFN_CONV - 1, f2), lambda s, i: (s, 0, 0))
    return pl.pallas_call(
        functools.partial(_conv_ffn_kernel, g=g, tt=tt),
        grid=(n_seq // g, t // tt),
        in_specs=[blk, st] + [_const_spec(a.shape) for a in (gn, wup, cw, cb, wdn)],
        out_specs=[blk, st],
        out_shape=[jax.ShapeDtypeStruct((n_seq, t, d), F32), jax.ShapeDtypeStruct((n_seq, FFN_CONV - 1, f2), F32)],
        scratch_shapes=[pltpu.VMEM((g, SUBLANES, f2), F32), pltpu.VMEM((g, SUBLANES + tt, fc), F32)],
        compiler_params=_params(2),
        name="conv_ffn",
    )(x3, prev, gn, wup, cw, cb, wdn)


def _block_diag(blocks):
    n, a, b = blocks.shape
    eye = jnp.eye(n, dtype=blocks.dtype)
    return (eye[:, None, :, None] * blocks[:, :, None, :]).reshape(n * a, n * b)


def _layer_weights(l, norm_mix, w_in, g_q, g_k, lam_q1, lam_k1, lam_q2, lam_k2, g_sub, lru_conv_w, lru_conv_b,
                   lru_wa, lru_ba, lru_wx, lru_bx, lru_lambda, w_o, norm_cross, norm_mem, w_cq, w_ck, w_cv,
                   g_cq, g_ck, w_co, norm_ffn, w_up, ffn_conv_w, ffn_conv_b, w_down):
    row = lambda a: a[l].reshape(1, -1)
    n_maps = DA_WIDTH // DA_HALF
    return dict(
        norm_mix=row(norm_mix), w_in=w_in[l].astype(BF16),
        pmat=jnp.kron(jnp.eye(n_maps, dtype=F32), jnp.full((DA_HALF, DA_HALF), 1.0 / DA_HALF, F32)).astype(BF16),
        gq=jnp.tile(g_q[l] * (DA_HALF ** -0.5), n_maps).reshape(1, -1),
        gq_base2=jnp.tile(g_q[l] * (DA_HALF ** -0.5 * LOG2E), n_maps).reshape(1, -1),
        gk=jnp.tile(g_k[l], n_maps).reshape(1, -1),
        lam=jnp.stack([lam_q1[l], lam_k1[l], lam_q2[l], lam_k2[l]]),
        gsub2=jnp.tile(g_sub[l], LANES // DA_HEAD).reshape(1, -1),
        gsub=row(g_sub),
        cw=lru_conv_w[l], cb=row(lru_conv_b),
        wa=_block_diag(lru_wa[l]).astype(BF16), ba=row(lru_ba),
        wx=_block_diag(lru_wx[l]).astype(BF16), bx=row(lru_bx), lru_lambda=row(lru_lambda),
        wo_da=w_o[l, :DA_WIDTH].astype(BF16), wo_lru=w_o[l, DA_WIDTH:].astype(BF16),
        norm_cross=row(norm_cross), norm_mem=row(norm_mem),
        wcq=w_cq[l].astype(BF16), wck=w_ck[l].astype(BF16), wcv=w_cv[l].astype(BF16),
        gcq=(g_cq[l] * (X_HEAD ** -0.5)).reshape(1, -1), gck=row(g_ck), wco=w_co[l].astype(BF16),
        norm_ffn=row(norm_ffn),
        ffn=_chunk_major(w_up[l].astype(BF16), ffn_conv_w[l], row(ffn_conv_b), w_down[l].astype(BF16)),
    )


def _mix(x3, da3, lx, lg, conv_prev, h0, mk3, mv3, mem_first, p, g, tt, mid_dtype):
    n_seq, t, d = x3.shape
    assert t >= LRU_CONV - 1 and t >= FFN_CONV - 1
    lru3, h_last = _lru(lx.reshape(n_seq, t, LRU_WIDTH), lg.reshape(n_seq, t, LRU_WIDTH), conv_prev, h0,
                        p['cw'], p['cb'], p['wa'], p['ba'], p['wx'], p['bx'], p['lru_lambda'], g, tt, mid_dtype)
    g_x = g
    while g_x > 1 and 4 * g_x * mk3.shape[1] * mk3.shape[2] * 4 > MEM_BLOCK_BYTES:
        g_x //= 2
    h3 = _out_cross(x3, da3, lru3, mk3, mv3, mem_first, p['wo_da'], p['wo_lru'], p['norm_cross'], p['wcq'], p['gcq'],
                    p['wco'], g_x, tt)
    conv_state = lx.reshape(n_seq, t, LRU_WIDTH)[:, t - (LRU_CONV - 1):]
    return h3, conv_state, h_last.reshape(n_seq, LRU_WIDTH)


def kernel(x_prompt, x_sample, mem_prompt, cache_k, cache_v, page_table, cache_mem_k, cache_mem_v, state_lru_conv, state_lru_h, state_ffn_conv, norm_mix, w_in, g_q, g_k, lam_q1, lam_k1, lam_q2, lam_k2, g_sub, lru_conv_w, lru_conv_b, lru_wa, lru_ba, lru_wx, lru_bx, lru_lambda, w_o, norm_cross, norm_mem, w_cq, w_ck, w_cv, g_cq, g_ck, w_co, norm_ffn, w_up, ffn_conv_w, ffn_conv_b, w_down):
    depth = w_in.shape[0]
    b, t, d = x_prompt.shape
    nb, nt, _ = x_sample.shape
    n_mem = mem_prompt.shape[1]
    n_phys, page = cache_k.shape[1], cache_k.shape[2]
    f2 = w_up.shape[2]
    to_pages = lambda c: jnp.transpose(c, (0, 1, 3, 4, 2)).reshape(depth * n_phys, N_HEADS_DA, DA_HEAD, page)
    cache_kt, cache_vt = to_pages(cache_k), to_pages(cache_v)
    to_heads = lambda a, n, m: jnp.transpose(a.reshape(n, m, N_HEADS_DA, DA_HEAD), (0, 2, 1, 3))
    cmk = cache_mem_k.reshape(depth * nb, n_mem * N_HEADS_X, X_HEAD)
    cmv = cache_mem_v.reshape(depth * nb, n_mem * N_HEADS_X, X_HEAD)

    tt_p = _tile(t, ROW_TILE)
    g_s = SAMPLE_SEQS if nb % SAMPLE_SEQS == 0 else 1

    yp, ys = x_prompt, x_sample
    outs_p, outs_s = [], []
    for l in range(depth):
        p = _layer_weights(l, norm_mix, w_in, g_q, g_k, lam_q1, lam_k1, lam_q2, lam_k2, g_sub, lru_conv_w,
                           lru_conv_b, lru_wa, lru_ba, lru_wx, lru_bx, lru_lambda, w_o, norm_cross, norm_mem,
                           w_cq, w_ck, w_cv, g_cq, g_ck, w_co, norm_ffn, w_up, ffn_conv_w, ffn_conv_b, w_down)
        lam_init = 0.8 - 0.6 * math.exp(-0.3 * l)

        q, kt, vt, lx, lg = _in_proj(yp.reshape(b * t, d), p['norm_mix'], p['w_in'], p['pmat'], p['gq_base2'], p['gk'],
                                     BF16, seq_len=t)
        da = _prompt_attn(q, kt, vt, p['lam'], p['gsub2'], lam_init)
        from_t = lambda a: jnp.transpose(a.reshape(b, N_HEADS_DA, DA_HEAD, t), (0, 3, 1, 2))
        mk, mv = _mem_kv(mem_prompt.reshape(b * n_mem, d), p['norm_mem'], p['wck'], p['wcv'], p['gck'])
        hp, conv_state_p, h_last_p = _mix(
            yp, da.reshape(b, t, DA_WIDTH), lx, lg,
            jnp.zeros((b, LRU_CONV - 1, LRU_WIDTH), F32), jnp.zeros((b, 1, LRU_WIDTH), F32),
            mk.reshape(b, n_mem, X_WIDTH), mv.reshape(b, n_mem, X_WIDTH), 0, p, 1, tt_p, BF16)
        ffn_p = (hp, jnp.zeros((b, FFN_CONV - 1, f2), F32), p['norm_ffn']) + p['ffn']

        q, k, v, lx, lg = _in_proj(ys.reshape(nb * nt, d), p['norm_mix'], p['w_in'], p['pmat'], p['gq'], p['gk'], F32)
        page_ids = page_table.reshape(-1).astype(jnp.int32) + l * n_phys
        attn_args = (page_ids, to_heads(q, nb, nt), to_heads(k, nb, nt), to_heads(v, nb, nt),
                     cache_kt, cache_vt, p['lam'], p['gsub'], lam_init)
        n_pages = page_table.shape[1]
        if _ffn_units(hp, tt_p, f2 // 2)[0] <= nb * (n_pages // _pages_per_step(n_pages)):
            da, yp, ffn_state_p = _sample_attn(*attn_args, ffn=ffn_p + (tt_p,))
        else:
            da = _sample_attn(*attn_args)
            yp, ffn_state_p = _conv_ffn(*ffn_p, 1, tt_p)
        outs_p.append((from_t(kt), from_t(vt),
                       mk.reshape(b, n_mem, N_HEADS_X, X_HEAD), mv.reshape(b, n_mem, N_HEADS_X, X_HEAD),
                       conv_state_p, h_last_p, ffn_state_p))
        da = jnp.transpose(da, (0, 2, 1, 3))
        hs, conv_state, h_last = _mix(
            ys, da.reshape(nb, nt, DA_WIDTH), lx, lg,
            state_lru_conv[l], state_lru_h[l].reshape(nb, 1, LRU_WIDTH),
            cmk, cmv, l * nb, p, g_s, nt, F32)
        ys, ffn_state = _conv_ffn(hs, state_ffn_conv[l], p['norm_ffn'], *p['ffn'], g_s, nt)
        outs_s.append((k.reshape(nb, nt, N_HEADS_DA, DA_HEAD), v.reshape(nb, nt, N_HEADS_DA, DA_HEAD),
                       conv_state, h_last, ffn_state))

    stack = lambda outs, j: jnp.stack([o[j] for o in outs])
    return (yp, ys, stack(outs_p, 0), stack(outs_p, 1), stack(outs_s, 0), stack(outs_s, 1),
            stack(outs_p, 2), stack(outs_p, 3), stack(outs_p, 4), stack(outs_s, 2),
            stack(outs_p, 5), stack(outs_s, 3), stack(outs_p, 6), stack(outs_s, 4))
```

```python
import functools
import math

import jax
import jax.numpy as jnp
from jax import lax
from jax.experimental import pallas as pl
from jax.experimental.pallas import tpu as pltpu

F32 = jnp.float32
BF16 = jnp.bfloat16

EPS = 1e-6
N_HEADS_DA = 8
DA_HEAD = 64
DA_HALF = DA_HEAD // 2
DA_WIDTH = N_HEADS_DA * DA_HEAD
LRU_WIDTH = 512
LRU_CONV = 4
LRU_C = 8.0
N_HEADS_X = 4
X_HEAD = 128
X_WIDTH = N_HEADS_X * X_HEAD
FFN_CONV = 3

LANES = 128
SUBLANES = 8
NEG = -1e30
VMEM_LIMIT = 48 * 1024 * 1024

ROW_TILE = 512
Q_TILE = 256
PAGES_PER_STEP = 16
PAGE_SLOTS = 3
FFN_CHUNK = 256
FUSED_VMEM_LIMIT = 56 * 1024 * 1024
SAMPLE_SEQS = 32
MEM_BLOCK_BYTES = 16 * 1024 * 1024


def _tile(n, pref):
    t = min(n, pref)
    while n % t:
        t -= SUBLANES
    return t


def _const_spec(shape):
    zeros = (0,) * len(shape)
    return pl.BlockSpec(shape, lambda *_: zeros, pipeline_mode=pl.Buffered(1))


def _params(n_axes):
    return pltpu.CompilerParams(dimension_semantics=("arbitrary",) * n_axes, vmem_limit_bytes=VMEM_LIMIT)


def _rmsnorm(x, g):
    return x * lax.rsqrt(jnp.mean(x * x, axis=-1, keepdims=True) + EPS) * g


def _sigmoid(x):
    return 0.5 * jnp.tanh(0.5 * x) + 0.5


def _dot(a, b):
    return jnp.dot(a, b, preferred_element_type=F32)


def _dot_nt(a, b):
    return lax.dot_general(a, b, (((1,), (1,)), ((), ())), preferred_element_type=F32)


def _diff_lambda(lam_ref, lam_init):
    v = lam_ref[...]
    s1 = jnp.sum(v[0:1] * v[1:2], axis=-1, keepdims=True)
    s2 = jnp.sum(v[2:3] * v[3:4], axis=-1, keepdims=True)
    return jnp.exp(s1) - jnp.exp(s2) + lam_init


def _in_proj_kernel(x_ref, gn_ref, w_ref, p_ref, gq_ref, gk_ref, q_ref, k_ref, v_ref, lx_ref, lg_ref, *scratch):
    xn = _rmsnorm(x_ref[...], gn_ref[...]).astype(BF16)

    def proj(j):
        return _dot(xn, w_ref[:, j * DA_WIDTH:(j + 1) * DA_WIDTH])

    def map_norm(z, g):
        sq = z * z
        hi = sq.astype(BF16)
        lo = (sq - hi.astype(F32)).astype(BF16)
        ms = _dot(hi, p_ref[...]) + _dot(lo, p_ref[...])
        return z * lax.rsqrt(ms + EPS) * g

    q_ref[...] = map_norm(proj(0), gq_ref[...]).astype(q_ref.dtype)
    k = map_norm(proj(1), gk_ref[...])
    if scratch:
        v_scr, = scratch
        v_scr[...] = proj(2)
        k_ref[...] = k.T
        v_ref[...] = v_scr[...].T
    else:
        k_ref[...] = k
        v_ref[...] = proj(2)
    lx_ref[...] = proj(3)
    lg_ref[...] = proj(4)


def _in_proj(x2d, gn, w_in, pmat, gq, gk, q_dtype, seq_len=None):
    n, d = x2d.shape
    r = _tile(seq_len or n, ROW_TILE)
    row = lambda w: pl.BlockSpec((r, w), lambda i: (i, 0))
    out_sd = lambda dt: jax.ShapeDtypeStruct((n, DA_WIDTH), dt)
    kv_spec, kv_sd = row(DA_WIDTH), out_sd(F32)
    if seq_len:
        tiles = seq_len // r
        kv_spec = pl.BlockSpec((None, DA_WIDTH, r), lambda i: (i // tiles, 0, i % tiles))
        kv_sd = jax.ShapeDtypeStruct((n // seq_len, DA_WIDTH, seq_len), F32)
    return pl.pallas_call(
        _in_proj_kernel,
        grid=(n // r,),
        in_specs=[row(d), _const_spec(gn.shape), _const_spec(w_in.shape), _const_spec(pmat.shape),
                  _const_spec(gq.shape), _const_spec(gk.shape)],
        out_specs=[row(DA_WIDTH), kv_spec, kv_spec, row(DA_WIDTH), row(DA_WIDTH)],
        out_shape=[out_sd(q_dtype), kv_sd, kv_sd, out_sd(F32), out_sd(F32)],
        scratch_shapes=[pltpu.VMEM((r, DA_WIDTH), F32)] if seq_len else [],
        compiler_params=_params(1),
        name="in_proj",
    )(x2d, gn, w_in, pmat, gq, gk)


def _mem_kv_kernel(m_ref, gn_ref, wk_ref, wv_ref, gck_ref, k_ref, v_ref):
    mn = _rmsnorm(m_ref[...], gn_ref[...]).astype(BF16)
    k = _dot(mn, wk_ref[...])
    for h in range(N_HEADS_X):
        sl = slice(h * X_HEAD, (h + 1) * X_HEAD)
        k_ref[:, sl] = _rmsnorm(k[:, sl], gck_ref[...])
    v_ref[...] = _dot(mn, wv_ref[...])


def _mem_kv(mem2d, gn, w_ck, w_cv, g_ck):
    n, d = mem2d.shape
    r = _tile(n, ROW_TILE)
    row = lambda w: pl.BlockSpec((r, w), lambda i: (i, 0))
    return pl.pallas_call(
        _mem_kv_kernel,
        grid=(n // r,),
        in_specs=[row(d), _const_spec(gn.shape), _const_spec(w_ck.shape), _const_spec(w_cv.shape),
                  _const_spec(g_ck.shape)],
        out_specs=[row(X_WIDTH)] * 2,
        out_shape=[jax.ShapeDtypeStruct((n, X_WIDTH), F32)] * 2,
        compiler_params=_params(1),
        name="mem_kv",
    )(mem2d, gn, w_ck, w_cv, g_ck)


def _head_slope(head, shape):
    e = (head + 1).astype(F32) * (-8.0 / N_HEADS_DA)
    return jnp.exp2(jnp.full(shape, e, F32))


def _sub_norm_pair(o, lane, gsub, lam_init):
    sq = o * o
    first = lane < DA_HEAD
    ms0 = jnp.sum(jnp.where(first, sq, 0.0), axis=-1, keepdims=True)
    ms1 = jnp.sum(jnp.where(first, 0.0, sq), axis=-1, keepdims=True)
    ms = jnp.where(first, ms0, ms1) * (1.0 / DA_HEAD)
    return (o * lax.rsqrt(ms + EPS) * gsub) * (1.0 - lam_init)


BIAS_SPLIT = 256


N_BIAS_TERMS = 3
LOG2E = math.log2(math.e)


def _prompt_attn_kernel(lam_ref, gsub_ref, q_ref, kt_ref, vt_ref, o_ref, kb_scr, vb_scr, *, tq, lam_init):
    hp = pl.program_id(1)
    t = kt_ref.shape[1]
    other = lambda hh: (1 - hh) * DA_HEAD

    slab_row = lax.broadcasted_iota(jnp.int32, (LANES, t), 0)
    pos = lax.broadcasted_iota(jnp.int32, (LANES, t), 1)
    pos_hi = (pos // BIAS_SPLIT * BIAS_SPLIT).astype(F32)
    pos_lo = (pos % BIAS_SPLIT).astype(F32)
    for hh in range(2):
        own = (slab_row >= hh * DA_HEAD) & (slab_row < (hh + 1) * DA_HEAD)
        rel = slab_row - other(hh)
        in_bias = (rel >= 0) & (rel < 2 * N_BIAS_TERMS)
        bias = jnp.where(in_bias, jnp.where(rel % 2 == 0, pos_hi, pos_lo), 0.0)
        kb_scr[hh] = jnp.where(own, kt_ref[...], bias).astype(BF16)
        vb_scr[hh] = jnp.where(own, vt_ref[...], 1.0).astype(BF16)

    lane = lax.broadcasted_iota(jnp.int32, (tq, LANES), 1)
    visible = (lax.broadcasted_iota(jnp.int32, (tq, tq), 1) <= lax.broadcasted_iota(jnp.int32, (tq, tq), 0))
    lam = _diff_lambda(lam_ref, lam_init)

    def q_slab(q, hh, c):
        lo = hh * DA_HEAD + c * DA_HALF
        rel = lane - other(hh)
        rest = _head_slope(hp * 2 + hh, (1, LANES)) * LOG2E
        factor = jnp.zeros((tq, LANES), F32)
        for piece in range(N_BIAS_TERMS):
            part = rest.astype(BF16).astype(F32)
            factor = jnp.where((rel >= 2 * piece) & (rel < 2 * piece + 2), part, factor)
            rest = rest - part
        return jnp.where((lane >= lo) & (lane < lo + DA_HALF), q, factor).astype(BF16)

    for qi in range(t // tq):
        rows = slice(qi * tq, (qi + 1) * tq)
        past = qi * tq
        q = q_ref[rows, :].astype(F32)
        heads = []
        for hh in range(2):
            own = (lane >= hh * DA_HEAD) & (lane < (hh + 1) * DA_HEAD)
            maps = []
            for c in range(2):
                qm = q_slab(q, hh, c)
                s_diag = jnp.where(visible, _dot(qm, kb_scr[hh, :, rows]), NEG)
                m = jnp.max(s_diag, axis=-1, keepdims=True)
                if past:
                    s_past = _dot(qm, kb_scr[hh, :, :past])
                    m = jnp.maximum(m, jnp.max(s_past, axis=-1, keepdims=True))
                acc = _dot_nt(jnp.exp2(s_diag - m).astype(BF16), vb_scr[hh, :, rows])
                if past:
                    acc = acc + _dot_nt(jnp.exp2(s_past - m).astype(BF16), vb_scr[hh, :, :past])
                maps.append(acc / jnp.where(own, pltpu.roll(acc, DA_HEAD, 1), 1.0))
            heads.append(maps[0] - lam * maps[1])
        o = jnp.where(lane < DA_HEAD, heads[0], heads[1])
        o_ref[rows, :] = _sub_norm_pair(o, lane, gsub_ref[...], lam_init).astype(o_ref.dtype)


def _prompt_attn(q, kt, vt, lam_vecs, gsub2, lam_init):
    b, _, t = kt.shape
    tq = _tile(t, Q_TILE)
    n_pairs = DA_WIDTH // LANES
    assert t <= BIAS_SPLIT * 256
    spec = pl.BlockSpec((t, LANES), lambda bi, hp: (bi, hp))
    spec_t = pl.BlockSpec((None, LANES, t), lambda bi, hp: (bi, hp, 0))
    return pl.pallas_call(
        functools.partial(_prompt_attn_kernel, tq=tq, lam_init=lam_init),
        grid=(b, n_pairs),
        in_specs=[_const_spec(lam_vecs.shape), _const_spec(gsub2.shape), spec, spec_t, spec_t],
        out_specs=spec,
        out_shape=jax.ShapeDtypeStruct((b * t, DA_WIDTH), BF16),
        scratch_shapes=[pltpu.VMEM((2, LANES, t), BF16), pltpu.VMEM((2, LANES, t), BF16)],
        compiler_params=_params(2),
        name="prompt_attn",
    )(lam_vecs, gsub2, q, kt, vt)


def _unit_of_step(step, n_units, n_steps):
    return (step * n_units) // n_steps


def _ffn_unit(refs, n_units, n_ch, tiles_per_seq, n_steps):
    (x_ref, prev_ref, gn_ref, wup_ref, cw_ref, cb_ref, wdn_ref, y_ref, state_ref, xn_scr, halo_scr, ext_scr) = refs
    hist = FFN_CONV - 1
    top = SUBLANES
    rows, d = x_ref.shape[1], x_ref.shape[2]
    step = pl.program_id(0) * pl.num_programs(1) + pl.program_id(1)
    ua = _unit_of_step(step, n_units, n_steps)
    active = _unit_of_step(step + 1, n_units, n_steps) > ua
    rt = ua // n_ch
    ch = ua % n_ch
    first_chunk = active & (ch == 0)

    @pl.when(first_chunk & (rt % tiles_per_seq == 0))
    def _():
        halo_scr[:, 0, top - hist:top, :] = prev_ref[0]

    @pl.when(first_chunk)
    def _():
        xn_scr[...] = _rmsnorm(x_ref[0], gn_ref[...]).astype(BF16)
        y_ref[...] = x_ref[...]

    halves = []

    def conv_half(half):
        j = half * n_ch + ch
        halves.append(_ffn_conv_half(xn_scr[...], 1, rows, wup_ref[j], cw_ref[j], cb_ref[j], halo_scr.at[j], ext_scr))

    def down():
        y_ref[...] += _dot(_ffn_gate(*halves), wdn_ref[ch]).reshape(1, rows, d)

    def finish():
        @pl.when(active & (rt % tiles_per_seq == tiles_per_seq - 1))
        def _():
            for h in range(2):
                state_ref[0, h * n_ch + ch] = halo_scr[h * n_ch + ch, 0, top - hist:top, :]

    return active, (functools.partial(conv_half, 0), functools.partial(conv_half, 1), down), finish


def _sample_attn_kernel(pt_ref, lam_ref, gsub_ref, q_ref, kn_ref, vn_ref, kc_ref, vc_ref, *refs,
                        n_pp, t, past, lam_init, ffn):
    no_stages = (lambda: None,) * 3
    if ffn:
        o_ref = refs[7]
        q_scr, m_scr, l_scr, acc_scr, kbuf, vbuf, sem = refs[10:17]
        ffn_active, ffn_stages, ffn_finish = _ffn_unit(refs[:7] + refs[8:10] + refs[17:], *ffn)
    else:
        o_ref, q_scr, m_scr, l_scr, acc_scr, kbuf, vbuf, sem = refs
    ci = pl.program_id(1)

    step = pl.program_id(0) * pl.num_programs(1) + ci
    n_steps = pl.num_programs(0) * pl.num_programs(1)
    ahead = PAGE_SLOTS - 1
    slot = step % PAGE_SLOTS

    def page_copies(of_step):
        into_slot = of_step % PAGE_SLOTS
        copies = []
        for i in range(n_pp):
            pid = pt_ref[of_step * n_pp + i]
            copies.append(pltpu.make_async_copy(kc_ref.at[pid], kbuf.at[into_slot, i], sem.at[into_slot, 0]))
            copies.append(pltpu.make_async_copy(vc_ref.at[pid], vbuf.at[into_slot, i], sem.at[into_slot, 1]))
        return copies

    for first in range(ahead):
        @pl.when((step == 0) & (first < n_steps))
        def _(first=first):
            for cp in page_copies(first):
                cp.start()

    @pl.when(step + ahead < n_steps)
    def _():
        for cp in page_copies(step + ahead):
            cp.start()

    for cp in page_copies(step):
        cp.wait()
    k_pages = [kbuf.at[slot, i] for i in range(n_pp)]
    v_pages = [vbuf.at[slot, i] for i in range(n_pp)]
    rows_h = 2 * t
    n_rows = N_HEADS_DA * rows_h
    tk = n_pp * kbuf.shape[-1]
    head_rows = lambda h: slice(h * rows_h, (h + 1) * rows_h)

    @pl.when(ci == 0)
    def _():
        first_map = lax.broadcasted_iota(jnp.int32, (t, DA_HEAD), 1) < DA_HALF
        for h in range(N_HEADS_DA):
            qh = q_ref[h]
            q_scr[h * rows_h:h * rows_h + t, :] = jnp.where(first_map, qh, 0.0)
            q_scr[h * rows_h + t:(h + 1) * rows_h, :] = jnp.where(first_map, 0.0, qh)
        m_scr[...] = jnp.full(m_scr.shape, NEG, F32)
        l_scr[...] = jnp.zeros(l_scr.shape, F32)
        acc_scr[...] = jnp.zeros(acc_scr.shape, F32)

    r_idx = lax.broadcasted_iota(jnp.int32, (n_rows, 1), 0)
    slope = jnp.exp2((r_idx // rows_h + 1).astype(F32) * (-8.0 / N_HEADS_DA))
    q_pos = r_idx % t

    def scores(keys_t):
        return jnp.concatenate(
            [_dot(q_scr[head_rows(h), :].astype(BF16), keys_t(h)) for h in range(N_HEADS_DA)], axis=0)

    def update(s, weighted_values, between=lambda: None):
        m_old = m_scr[...]
        m_new = jnp.maximum(m_old, jnp.max(s, axis=-1, keepdims=True))
        alpha = jnp.exp(m_old - m_new)
        p = jnp.exp(s - m_new)
        l_scr[...] = alpha * l_scr[...] + jnp.sum(p, axis=-1, keepdims=True)
        pb = p.astype(BF16)
        between()
        pv = jnp.concatenate([weighted_values(h, pb[head_rows(h), :]) for h in range(N_HEADS_DA)], axis=0)
        acc_scr[...] = alpha * acc_scr[...] + pv
        m_scr[...] = m_new

    def cached_pages(stages):
        key_off = lax.broadcasted_iota(jnp.int32, (1, tk), 1) + (ci * tk - past)
        s = scores(lambda h: jnp.concatenate([kp[h] for kp in k_pages], axis=1).astype(BF16))
        s = s - slope * (q_pos - key_off).astype(F32)
        stages[0]()
        update(s, lambda h, ph: _dot_nt(ph, jnp.concatenate([vp[h] for vp in v_pages], axis=1).astype(BF16)),
               between=stages[1])
        stages[2]()

    if ffn:
        pl.when(ffn_active)(lambda: cached_pages(ffn_stages))
        pl.when(jnp.logical_not(ffn_active))(lambda: cached_pages(no_stages))
        ffn_finish()
    else:
        cached_pages(no_stages)

    @pl.when(ci == pl.num_programs(1) - 1)
    def _():
        pad = jnp.zeros((LANES - t, DA_HEAD), F32)
        padded = lambda ref, h: jnp.concatenate([ref[h], pad], axis=0).astype(BF16)
        new_i = lax.broadcasted_iota(jnp.int32, (1, LANES), 1)
        s = jnp.concatenate(
            [_dot_nt(q_scr[head_rows(h), :].astype(BF16), padded(kn_ref, h)) for h in range(N_HEADS_DA)], axis=0)
        s = s - slope * (q_pos - new_i).astype(F32)
        s = jnp.where((new_i < t) & (new_i <= q_pos), s, NEG)
        update(s, lambda h, ph: _dot(ph, padded(vn_ref, h)))

        o = acc_scr[...] / l_scr[...]
        lam = _diff_lambda(lam_ref, lam_init)
        for h in range(N_HEADS_DA):
            d = o[h * rows_h:h * rows_h + t, :] - lam * o[h * rows_h + t:(h + 1) * rows_h, :]
            o_ref[h] = _rmsnorm(d, gsub_ref[...]) * (1.0 - lam_init)


def _pages_per_step(n_pages):
    n_pp = min(PAGES_PER_STEP, n_pages)
    while n_pages % n_pp:
        n_pp -= 1
    return n_pp


def _ffn_units(x3, tt, f):
    n_ch = f // _tile(f, FFN_CHUNK)
    return x3.shape[0] * (x3.shape[1] // tt) * n_ch, n_ch, x3.shape[1] // tt


def _sample_attn(page_ids, q4, k_new4, v_new4, cache_kt, cache_vt, lam_vecs, gsub, lam_init, ffn=None):
    n_seq, n_heads, t, dh = q4.shape
    page = cache_kt.shape[-1]
    n_pages = page_ids.shape[0] // n_seq
    n_pp = _pages_per_step(n_pages)
    n_rows = 2 * n_heads * t
    assert (2 * t) % 16 == 0 and t <= LANES

    seq_spec = pl.BlockSpec((None, n_heads, t, dh), lambda s, c, pt: (s, 0, 0, 0))
    const = lambda shape: pl.BlockSpec(shape, lambda s, c, pt: (0,) * len(shape), pipeline_mode=pl.Buffered(1))
    n_chunks = n_pages // n_pp
    in_specs = ([const(lam_vecs.shape), const(gsub.shape), seq_spec, seq_spec, seq_spec]
                + [pl.BlockSpec(memory_space=pl.ANY)] * 2)
    out_specs = [seq_spec]
    out_shape = [jax.ShapeDtypeStruct((n_seq, n_heads, t, dh), F32)]
    scratch = [pltpu.VMEM((n_rows, dh), F32), pltpu.VMEM((n_rows, 1), F32),
               pltpu.VMEM((n_rows, 1), F32), pltpu.VMEM((n_rows, dh), F32),
               pltpu.VMEM((PAGE_SLOTS, n_pp, n_heads, dh, page), F32),
               pltpu.VMEM((PAGE_SLOTS, n_pp, n_heads, dh, page), F32), pltpu.SemaphoreType.DMA((PAGE_SLOTS, 2))]
    operands = [page_ids, lam_vecs, gsub, q4, k_new4, v_new4, cache_kt, cache_vt]
    ffn_static = None
    if ffn is not None:
        x3, prev, gn, wup, cw, cb, wdn, tt = ffn
        b, _, d = x3.shape
        n_ch, fc, _ = wdn.shape
        n_units, _, tiles = _ffn_units(x3, tt, n_ch * fc)
        n_steps = n_seq * n_chunks
        assert n_units <= n_steps
        ffn_static = (n_units, n_ch, tiles, n_steps)
        prev_c = jnp.transpose(prev.reshape(b, FFN_CONV - 1, 2 * n_ch, fc), (0, 2, 1, 3))

        def tile_of(s, c):
            return _unit_of_step(s * n_chunks + c, n_units, n_steps) // n_ch

        x_spec = pl.BlockSpec((1, tt, d), lambda s, c, pt: (tile_of(s, c) // tiles, tile_of(s, c) % tiles, 0))
        st_spec = pl.BlockSpec((1, 2 * n_ch, FFN_CONV - 1, fc), lambda s, c, pt: (tile_of(s, c) // tiles, 0, 0, 0))
        ffn_in = [x3, prev_c, gn, wup, cw, cb, wdn]
        in_specs += [x_spec, st_spec] + [const(a.shape) for a in ffn_in[2:]]
        out_specs += [x_spec, st_spec]
        out_shape += [jax.ShapeDtypeStruct(x3.shape, F32), jax.ShapeDtypeStruct(prev_c.shape, F32)]
        scratch += [pltpu.VMEM((tt, d), BF16), pltpu.VMEM((2 * n_ch, 1, SUBLANES, fc), F32),
                    pltpu.VMEM((1, SUBLANES + tt, fc), F32)]
        operands += ffn_in
    grid_spec = pltpu.PrefetchScalarGridSpec(
        num_scalar_prefetch=1, grid=(n_seq, n_chunks), in_specs=in_specs, out_specs=out_specs, scratch_shapes=scratch)
    outs = pl.pallas_call(
        functools.partial(_sample_attn_kernel, n_pp=n_pp, t=t, past=n_pages * page, lam_init=lam_init,
                          ffn=ffn_static),
        grid_spec=grid_spec,
        out_shape=out_shape,
        compiler_params=pltpu.CompilerParams(dimension_semantics=("arbitrary",) * 2,
                                             vmem_limit_bytes=FUSED_VMEM_LIMIT if ffn is not None else VMEM_LIMIT),
        name="sample_attn",
    )(*operands)
    if ffn is None:
        return outs[0]
    state = jnp.transpose(outs[2], (0, 2, 1, 3)).reshape(prev.shape)
    return outs[0], outs[1], state


def _lru_kernel(lx_ref, lg_ref, cprev_ref, h0_ref, cw_ref, cb_ref, wa_ref, ba_ref, wx_ref, bx_ref, lam_ref,
                out_ref, hlast_ref, ext_scr, h_scr, a_scr, g_scr, *, g, tt):
    i = pl.program_id(1)
    w = LRU_WIDTH
    hist = LRU_CONV - 1
    top = SUBLANES

    @pl.when(i == 0)
    def _():
        ext_scr[:, top - hist:top, :] = cprev_ref[...]
        h_scr[...] = h0_ref[...]

    ext_scr[:, top:top + tt, :] = lx_ref[...]
    nc = tt // SUBLANES
    ext4 = ext_scr[...].reshape(g, nc + 1, SUBLANES, w)
    sub4 = lax.broadcasted_iota(jnp.int32, (g, nc, SUBLANES, w), 2)
    xc4 = cb_ref[...]
    for j in range(LRU_CONV):
        k = hist - j
        if k:
            rolled = pltpu.roll(ext4, k, 2)
            shifted = jnp.where(sub4 >= k, rolled[:, 1:], rolled[:, :nc])
        else:
            shifted = ext4[:, 1:]
        xc4 = xc4 + shifted * cw_ref[j:j + 1, :]
    ext_scr[:, top - hist:top, :] = ext_scr[:, top + tt - hist:top + tt, :]

    r = g * tt
    xc = xc4.reshape(r, w)
    xb = xc.astype(BF16)
    rg = _sigmoid(_dot(xb, wa_ref[...]) + ba_ref[...])
    ig = _sigmoid(_dot(xb, wx_ref[...]) + bx_ref[...])
    nl = -lam_ref[...]
    softplus = jnp.maximum(nl, 0.0) + jnp.log1p(jnp.exp(-jnp.abs(nl)))
    log_a = -LRU_C * rg * softplus
    a = jnp.exp(log_a)
    one_minus_a2 = -jnp.tanh(log_a) * (a * a + 1.0)
    gx = jnp.where(one_minus_a2 > 0.0, one_minus_a2 * lax.rsqrt(one_minus_a2), 0.0) * (ig * xc)

    nch = r // SUBLANES
    a3 = a.reshape(nch, SUBLANES, w)
    g3 = gx.reshape(nch, SUBLANES, w)
    sub = lax.broadcasted_iota(jnp.int32, (nch, SUBLANES, w), 1)
    for s in (1, 2, 4):
        keep = sub >= s
        a_prev = jnp.where(keep, pltpu.roll(a3, s, 1), 1.0)
        g_prev = jnp.where(keep, pltpu.roll(g3, s, 1), 0.0)
        g3 = a3 * g_prev + g3
        a3 = a3 * a_prev

    nc = tt // SUBLANES
    a_scr[...] = a3.reshape(g, nc, SUBLANES, w)
    g_scr[...] = g3.reshape(g, nc, SUBLANES, w)

    def chunk(c, h):
        hs = a_scr[:, c] * h + g_scr[:, c]
        g_scr[:, c] = hs
        return hs[:, SUBLANES - 1:SUBLANES, :]

    h = h_scr[...]
    if nc <= 2:
        for c in range(nc):
            h = chunk(c, h)
    else:
        h = lax.fori_loop(0, nc, chunk, h)
    h_scr[...] = h

    hs = g_scr[...].reshape(r, w)
    out = hs * jax.nn.gelu(lg_ref[...].reshape(r, w))
    out_ref[...] = out.reshape(g, tt, w).astype(out_ref.dtype)

    @pl.when(i == pl.num_programs(1) - 1)
    def _():
        hlast_ref[...] = h


def _lru(lx3, lg3, conv_prev, h0, cw, cb, wa, ba, wx, bx, lam, g, tt, out_dtype):
    n_seq, t, w = lx3.shape
    blk = pl.BlockSpec((g, tt, w), lambda s, i: (s, i, 0))
    seq = lambda rows: pl.BlockSpec((g, rows, w), lambda s, i: (s, 0, 0))
    return pl.pallas_call(
        functools.partial(_lru_kernel, g=g, tt=tt),
        grid=(n_seq // g, t // tt),
        in_specs=[blk, blk, seq(LRU_CONV - 1), seq(1)] + [_const_spec(a.shape) for a in (cw, cb, wa, ba, wx, bx, lam)],
        out_specs=[blk, seq(1)],
        out_shape=[jax.ShapeDtypeStruct((n_seq, t, w), out_dtype), jax.ShapeDtypeStruct((n_seq, 1, w), F32)],
        scratch_shapes=[pltpu.VMEM((g, SUBLANES + tt, w), F32), pltpu.VMEM((g, 1, w), F32),
                        pltpu.VMEM((g, tt // SUBLANES, SUBLANES, w), F32),
                        pltpu.VMEM((g, tt // SUBLANES, SUBLANES, w), F32)],
        compiler_params=_params(2),
        name="lru",
    )(lx3, lg3, conv_prev, h0, cw, cb, wa, ba, wx, bx, lam)


def _out_cross_kernel(x_ref, da_ref, lru_ref, mk_ref, mv_ref, wo_da_ref, wo_lru_ref, gn_ref, wcq_ref, gcq_ref,
                      wco_ref, out_ref, q_scr, o_scr, *ring, g, tt, mem_first):
    r = g * tt
    d = x_ref.shape[-1]
    if ring:
        kbuf, vbuf, sem = ring
        step = pl.program_id(0)
        n_steps = pl.num_programs(0)
        ahead = PAGE_SLOTS - 1
        mk_hbm, mv_hbm = mk_ref, mv_ref

        def mem_copies(of_step):
            slot = of_step % PAGE_SLOTS
            src = pl.ds(mem_first + of_step * g, g)
            return [pltpu.make_async_copy(mk_hbm.at[src], kbuf.at[slot], sem.at[slot, 0]),
                    pltpu.make_async_copy(mv_hbm.at[src], vbuf.at[slot], sem.at[slot, 1])]

        for first in range(ahead):
            @pl.when((step == 0) & (first < n_steps))
            def _(first=first):
                for cp in mem_copies(first):
                    cp.start()

        @pl.when(step + ahead < n_steps)
        def _():
            for cp in mem_copies(step + ahead):
                cp.start()

        mk_ref, mv_ref = kbuf.at[step % PAGE_SLOTS], vbuf.at[step % PAGE_SLOTS]
    h = (x_ref[...].reshape(r, d)
         + _dot(da_ref[...].reshape(r, DA_WIDTH).astype(BF16), wo_da_ref[...])
         + _dot(lru_ref[...].reshape(r, LRU_WIDTH).astype(BF16), wo_lru_ref[...]))
    hn = _rmsnorm(h, gn_ref[...]).astype(BF16)
    qc = _dot(hn, wcq_ref[...])
    for hd in range(N_HEADS_X):
        sl = slice(hd * X_HEAD, (hd + 1) * X_HEAD)
        q_scr[:, sl] = _rmsnorm(qc[:, sl], gcq_ref[...])

    if mk_ref.shape[-1] == X_WIDTH:
        for hd in range(N_HEADS_X):
            sl = slice(hd * X_HEAD, (hd + 1) * X_HEAD)
            s = _dot_nt(q_scr[:, sl].astype(BF16), mk_ref[0, :, sl].astype(BF16))
            e = jnp.exp(s - jnp.max(s, axis=-1, keepdims=True))
            o = _dot(e.astype(BF16), mv_ref[0, :, sl].astype(BF16))
            o_scr[:, sl] = o / jnp.sum(e, axis=-1, keepdims=True)
    else:
        n_rows = N_HEADS_X * tt
        n_cols = mk_ref.shape[1]
        col_head = lax.broadcasted_iota(jnp.int32, (n_rows, n_cols), 1) % N_HEADS_X
        own_head = col_head == lax.broadcasted_iota(jnp.int32, (n_rows, n_cols), 0) // tt
        if ring:
            for cp in mem_copies(step):
                cp.wait()

        def one_seq(si, carry):
            r0 = pl.multiple_of(si * tt, tt)
            q = q_scr[pl.ds(r0, tt), :]
            qs = jnp.concatenate([q[:, hd * X_HEAD:(hd + 1) * X_HEAD] for hd in range(N_HEADS_X)], axis=0)
            s = jnp.where(own_head, _dot_nt(qs.astype(BF16), mk_ref[si].astype(BF16)), NEG)
            e = jnp.exp(s - jnp.max(s, axis=-1, keepdims=True))
            o = _dot(e.astype(BF16), mv_ref[si].astype(BF16)) / jnp.sum(e, axis=-1, keepdims=True)
            for hd in range(N_HEADS_X):
                o_scr[pl.ds(r0, tt), hd * X_HEAD:(hd + 1) * X_HEAD] = o[hd * tt:(hd + 1) * tt, :]
            return carry

        lax.fori_loop(0, g, one_seq, 0, unroll=4 if g % 4 == 0 else 1)

    out = h + _dot(o_scr[...].astype(BF16), wco_ref[...])
    out_ref[...] = out.reshape(g, tt, d)


def _out_cross(x3, da3, lru3, mk3, mv3, mem_first, wo_da, wo_lru, gn, wcq, gcq, wco, g, tt):
    n_seq, t, d = x3.shape
    assert mk3.shape[-1] == X_HEAD or (mk3.shape[-1] == X_WIDTH and g == 1)
    mem_blk0 = mem_first // g
    blk = lambda w: pl.BlockSpec((g, tt, w), lambda s, i: (s, i, 0))
    mem = pl.BlockSpec((g,) + mk3.shape[1:], lambda s, i: (s + mem_blk0, 0, 0))
    scratch = [pltpu.VMEM((g * tt, X_WIDTH), F32), pltpu.VMEM((g * tt, X_WIDTH), F32)]
    if mk3.shape[-1] == X_HEAD and t == tt:
        mem = pl.BlockSpec(memory_space=pl.ANY)
        scratch += [pltpu.VMEM((PAGE_SLOTS, g) + mk3.shape[1:], F32), pltpu.VMEM((PAGE_SLOTS, g) + mk3.shape[1:], F32),
                    pltpu.SemaphoreType.DMA((PAGE_SLOTS, 2))]
    return pl.pallas_call(
        functools.partial(_out_cross_kernel, g=g, tt=tt, mem_first=mem_first),
        grid=(n_seq // g, t // tt),
        in_specs=[blk(d), blk(DA_WIDTH), blk(LRU_WIDTH), mem, mem]
                 + [_const_spec(a.shape) for a in (wo_da, wo_lru, gn, wcq, gcq, wco)],
        out_specs=blk(d),
        out_shape=jax.ShapeDtypeStruct((n_seq, t, d), F32),
        scratch_shapes=scratch,
        compiler_params=_params(2),
        name="out_cross",
    )(x3, da3, lru3, mk3, mv3, wo_da, wo_lru, gn, wcq, gcq, wco)


def _ffn_conv_half(xn, g, tt, up_w, conv_w, conv_b, h_ref, ext_scr):
    hist = FFN_CONV - 1
    top = SUBLANES
    fc = ext_scr.shape[-1]
    ext_scr[:, top:top + tt, :] = _dot(xn, up_w).reshape(g, tt, fc)
    ext_scr[:, top - hist:top, :] = h_ref[:, top - hist:top, :]
    c3 = conv_b
    for j in range(FFN_CONV):
        c3 = c3 + ext_scr[:, top - hist + j:top - hist + j + tt, :] * conv_w[j:j + 1, :]
    h_ref[:, top - hist:top, :] = ext_scr[:, top + tt - hist:top + tt, :]
    return c3.reshape(g * tt, fc)


def _ffn_gate(gate, value):
    return (gate * _sigmoid(gate) * value).astype(BF16)


def _conv_ffn_kernel(x_ref, prev_ref, gn_ref, wup_ref, cw_ref, cb_ref, wdn_ref, out_ref, state_ref,
                     halo_scr, ext_scr, *, g, tt):
    i = pl.program_id(1)
    r = g * tt
    d = x_ref.shape[-1]
    n_ch, fc, _ = wdn_ref.shape
    f = n_ch * fc
    hist = FFN_CONV - 1
    top = SUBLANES

    @pl.when(i == 0)
    def _():
        halo_scr[:, top - hist:top, :] = prev_ref[...]

    xn = _rmsnorm(x_ref[...].reshape(r, d), gn_ref[...]).astype(BF16)
    out_ref[...] = x_ref[...]

    for ch in range(n_ch):
        halves = []
        for half in range(2):
            j = half * n_ch + ch
            cols = slice(half * f + ch * fc, half * f + (ch + 1) * fc)
            halves.append(_ffn_conv_half(xn, g, tt, wup_ref[j], cw_ref[j], cb_ref[j], halo_scr.at[:, :, cols], ext_scr))
        out_ref[...] += _dot(_ffn_gate(*halves), wdn_ref[ch]).reshape(g, tt, d)

    @pl.when(i == pl.num_programs(1) - 1)
    def _():
        state_ref[...] = halo_scr[:, top - hist:top, :]


def _chunk_major(wup, cw, cb, wdn):
    f, d = wdn.shape
    fc = _tile(f, FFN_CHUNK)
    n_ch = f // fc
    chunked = lambda a: jnp.transpose(a.reshape(a.shape[0], 2, n_ch, fc), (1, 2, 0, 3)).reshape(
        2 * n_ch, a.shape[0], fc)
    return chunked(wup), chunked(cw), chunked(cb), wdn.reshape(n_ch, fc, d)


def _conv_ffn(x3, prev, gn, wup, cw, cb, wdn, g, tt):
    n_seq, t, d = x3.shape
    n_ch, fc, _ = wdn.shape
    f2 = 2 * n_ch * fc
    blk = pl.BlockSpec((g, tt, d), lambda s, i: (s, i, 0))
    st = pl.BlockSpec((g, FFN_CONV - 1, f2), lambda s, i: (s, 0, 0))
    return pl.pallas_call(
        functools.partial(_conv_ffn_kernel, g=g, tt=tt),
        grid=(n_seq // g, t // tt),
        in_specs=[blk, st] + [_const_spec(a.shape) for a in (gn, wup, cw, cb, wdn)],
        out_specs=[blk, st],
        out_shape=[jax.ShapeDtypeStruct((n_seq, t, d), F32), jax.ShapeDtypeStruct((n_seq, FFN_CONV - 1, f2), F32)],
        scratch_shapes=[pltpu.VMEM((g, SUBLANES, f2), F32), pltpu.VMEM((g, SUBLANES + tt, fc), F32)],
        compiler_params=_params(2),
        name="conv_ffn",
    )(x3, prev, gn, wup, cw, cb, wdn)


def _block_diag(blocks):
    n, a, b = blocks.shape
    eye = jnp.eye(n, dtype=blocks.dtype)
    return (eye[:, None, :, None] * blocks[:, :, None, :]).reshape(n * a, n * b)


def _layer_weights(l, norm_mix, w_in, g_q, g_k, lam_q1, lam_k1, lam_q2, lam_k2, g_sub, lru_conv_w, lru_conv_b,
                   lru_wa, lru_ba, lru_wx, lru_bx, lru_lambda, w_o, norm_cross, norm_mem, w_cq, w_ck, w_cv,
                   g_cq, g_ck, w_co, norm_ffn, w_up, ffn_conv_w, ffn_conv_b, w_down):
    row = lambda a: a[l].reshape(1, -1)
    n_maps = DA_WIDTH // DA_HALF
    return dict(
        norm_mix=row(norm_mix), w_in=w_in[l].astype(BF16),
        pmat=jnp.kron(jnp.eye(n_maps, dtype=F32), jnp.full((DA_HALF, DA_HALF), 1.0 / DA_HALF, F32)).astype(BF16),
        gq=jnp.tile(g_q[l] * (DA_HALF ** -0.5), n_maps).reshape(1, -1),
        gq_base2=jnp.tile(g_q[l] * (DA_HALF ** -0.5 * LOG2E), n_maps).reshape(1, -1),
        gk=jnp.tile(g_k[l], n_maps).reshape(1, -1),
        lam=jnp.stack([lam_q1[l], lam_k1[l], lam_q2[l], lam_k2[l]]),
        gsub2=jnp.tile(g_sub[l], LANES // DA_HEAD).reshape(1, -1),
        gsub=row(g_sub),
        cw=lru_conv_w[l], cb=row(lru_conv_b),
        wa=_block_diag(lru_wa[l]).astype(BF16), ba=row(lru_ba),
        wx=_block_diag(lru_wx[l]).astype(BF16), bx=row(lru_bx), lru_lambda=row(lru_lambda),
        wo_da=w_o[l, :DA_WIDTH].astype(BF16), wo_lru=w_o[l, DA_WIDTH:].astype(BF16),
        norm_cross=row(norm_cross), norm_mem=row(norm_mem),
        wcq=w_cq[l].astype(BF16), wck=w_ck[l].astype(BF16), wcv=w_cv[l].astype(BF16),
        gcq=(g_cq[l] * (X_HEAD ** -0.5)).reshape(1, -1), gck=row(g_ck), wco=w_co[l].astype(BF16),
        norm_ffn=row(norm_ffn),
        ffn=_chunk_major(w_up[l].astype(BF16), ffn_conv_w[l], row(ffn_conv_b), w_down[l].astype(BF16)),
    )


def _mix(x3, da3, lx, lg, conv_prev, h0, mk3, mv3, mem_first, p, g, tt, mid_dtype):
    n_seq, t, d = x3.shape
    assert t >= LRU_CONV - 1 and t >= FFN_CONV - 1
    lru3, h_last = _lru(lx.reshape(n_seq, t, LRU_WIDTH), lg.reshape(n_seq, t, LRU_WIDTH), conv_prev, h0,
                        p['cw'], p['cb'], p['wa'], p['ba'], p['wx'], p['bx'], p['lru_lambda'], g, tt, mid_dtype)
    g_x = g
    while g_x > 1 and 4 * g_x * mk3.shape[1] * mk3.shape[2] * 4 > MEM_BLOCK_BYTES:
        g_x //= 2
    h3 = _out_cross(x3, da3, lru3, mk3, mv3, mem_first, p['wo_da'], p['wo_lru'], p['norm_cross'], p['wcq'], p['gcq'],
                    p['wco'], g_x, tt)
    conv_state = lx.reshape(n_seq, t, LRU_WIDTH)[:, t - (LRU_CONV - 1):]
    return h3, conv_state, h_last.reshape(n_seq, LRU_WIDTH)


def kernel(x_prompt, x_sample, mem_prompt, cache_k, cache_v, page_table, cache_mem_k, cache_mem_v, state_lru_conv, state_lru_h, state_ffn_conv, norm_mix, w_in, g_q, g_k, lam_q1, lam_k1, lam_q2, lam_k2, g_sub, lru_conv_w, lru_conv_b, lru_wa, lru_ba, lru_wx, lru_bx, lru_lambda, w_o, norm_cross, norm_mem, w_cq, w_ck, w_cv, g_cq, g_ck, w_co, norm_ffn, w_up, ffn_conv_w, ffn_conv_b, w_down):
    depth = w_in.shape[0]
    b, t, d = x_prompt.shape
    nb, nt, _ = x_sample.shape
    n_mem = mem_prompt.shape[1]
    n_phys, page = cache_k.shape[1], cache_k.shape[2]
    f2 = w_up.shape[2]
    to_pages = lambda c: jnp.transpose(c, (0, 1, 3, 4, 2)).reshape(depth * n_phys, N_HEADS_DA, DA_HEAD, page)
    cache_kt, cache_vt = to_pages(cache_k), to_pages(cache_v)
    to_heads = lambda a, n, m: jnp.transpose(a.reshape(n, m, N_HEADS_DA, DA_HEAD), (0, 2, 1, 3))
    cmk = cache_mem_k.reshape(depth * nb, n_mem * N_HEADS_X, X_HEAD)
    cmv = cache_mem_v.reshape(depth * nb, n_mem * N_HEADS_X, X_HEAD)

    tt_p = _tile(t, ROW_TILE)
    g_s = SAMPLE_SEQS if nb % SAMPLE_SEQS == 0 else 1

    yp, ys = x_prompt, x_sample
    outs_p, outs_s = [], []
    for l in range(depth):
        p = _layer_weights(l, norm_mix, w_in, g_q, g_k, lam_q1, lam_k1, lam_q2, lam_k2, g_sub, lru_conv_w,
                           lru_conv_b, lru_wa, lru_ba, lru_wx, lru_bx, lru_lambda, w_o, norm_cross, norm_mem,
                           w_cq, w_ck, w_cv, g_cq, g_ck, w_co, norm_ffn, w_up, ffn_conv_w, ffn_conv_b, w_down)
        lam_init = 0.8 - 0.6 * math.exp(-0.3 * l)

        q, kt, vt, lx, lg = _in_proj(yp.reshape(b * t, d), p['norm_mix'], p['w_in'], p['pmat'], p['gq_base2'], p['gk'],
                                     BF16, seq_len=t)
        da = _prompt_attn(q, kt, vt, p['lam'], p['gsub2'], lam_init)
        from_t = lambda a: jnp.transpose(a.reshape(b, N_HEADS_DA, DA_HEAD, t), (0, 3, 1, 2))
        mk, mv = _mem_kv(mem_prompt.reshape(b * n_mem, d), p['norm_mem'], p['wck'], p['wcv'], p['gck'])
        hp, conv_state_p, h_last_p = _mix(
            yp, da.reshape(b, t, DA_WIDTH), lx, lg,
            jnp.zeros((b, LRU_CONV - 1, LRU_WIDTH), F32), jnp.zeros((b, 1, LRU_WIDTH), F32),
            mk.reshape(b, n_mem, X_WIDTH), mv.reshape(b, n_mem, X_WIDTH), 0, p, 1, tt_p, BF16)
        ffn_p = (hp, jnp.zeros((b, FFN_CONV - 1, f2), F32), p['norm_ffn']) + p['ffn']

        q, k, v, lx, lg = _in_proj(ys.reshape(nb * nt, d), p['norm_mix'], p['w_in'], p['pmat'], p['gq'], p['gk'], F32)
        page_ids = page_table.reshape(-1).astype(jnp.int32) + l * n_phys
        attn_args = (page_ids, to_heads(q, nb, nt), to_heads(k, nb, nt), to_heads(v, nb, nt),
                     cache_kt, cache_vt, p['lam'], p['gsub'], lam_init)
        n_pages = page_table.shape[1]
        if _ffn_units(hp, tt_p, f2 // 2)[0] <= nb * (n_pages // _pages_per_step(n_pages)):
            da, yp, ffn_state_p = _sample_attn(*attn_args, ffn=ffn_p + (tt_p,))
        else:
            da = _sample_attn(*attn_args)
            yp, ffn_state_p = _conv_ffn(*ffn_p, 1, tt_p)
        outs_p.append((from_t(kt), from_t(vt),
                       mk.reshape(b, n_mem, N_HEADS_X, X_HEAD), mv.reshape(b, n_mem, N_HEADS_X, X_HEAD),
                       conv_state_p, h_last_p, ffn_state_p))
        da = jnp.transpose(da, (0, 2, 1, 3))
        hs, conv_state, h_last = _mix(
            ys, da.reshape(nb, nt, DA_WIDTH), lx, lg,
            state_lru_conv[l], state_lru_h[l].reshape(nb, 1, LRU_WIDTH),
            cmk, cmv, l * nb, p, g_s, nt, F32)
        ys, ffn_state = _conv_ffn(hs, state_ffn_conv[l], p['norm_ffn'], *p['ffn'], g_s, nt)
        outs_s.append((k.reshape(nb, nt, N_HEADS_DA, DA_HEAD), v.reshape(nb, nt, N_HEADS_DA, DA_HEAD),
                       conv_state, h_last, ffn_state))

    stack = lambda outs, j: jnp.stack([o[j] for o in outs])
    return (yp, ys, stack(outs_p, 0), stack(outs_p, 1), stack(outs_s, 0), stack(outs_s, 1),
            stack(outs_p, 2), stack(outs_p, 3), stack(outs_p, 4), stack(outs_s, 2),
            stack(outs_p, 5), stack(outs_s, 3), stack(outs_p, 6), stack(outs_s, 4))
```
